```python
import jax, jax.numpy as jnp
from jax import lax
import numpy as np

D_MODEL = 2048
BATCH = 2
SEQ = 4096
DEPTH = 2

CHUNK = 64
Q_BLOCK = 128
N_A = DEPTH // 2
N_B = DEPTH - N_A
PLE_DIM = 256
RMS_EPS = 1e-6
NEG_INF = -1e30

SB_HEAD_DIM = 128
SB_HEADS = D_MODEL // SB_HEAD_DIM
SB_WIDTH = SB_HEADS * SB_HEAD_DIM

QK_NOPE_DIM = 128
QK_ROPE_DIM = 64
V_HEAD_DIM = 128
MLA_HEADS = D_MODEL // V_HEAD_DIM
Q_LORA_RANK = D_MODEL // 4
KV_LORA_RANK = D_MODEL // 4
MLA_WIDTH = MLA_HEADS * V_HEAD_DIM
QK_HEAD_DIM = QK_NOPE_DIM + QK_ROPE_DIM
ROPE_THETA = 10000.0

kernel_name = "yoco_stickbreak_mla_hybrid"


def rms_norm(x, g):
    xf = x.astype(jnp.float32)
    y = xf * lax.rsqrt(jnp.mean(xf * xf, axis=-1, keepdims=True) + RMS_EPS)
    return (y * g.astype(jnp.float32)).astype(x.dtype)


def rope_tables(seq_len):
    inv = 1.0 / (ROPE_THETA ** (jnp.arange(0, QK_ROPE_DIM, 2, dtype=jnp.float32) / QK_ROPE_DIM))
    ang = jnp.arange(seq_len, dtype=jnp.float32)[:, None] * inv[None, :]
    return jnp.cos(ang), jnp.sin(ang)


def apply_rope(t, cos, sin):
    tf = t.astype(jnp.float32)
    t1, t2 = jnp.split(tf, 2, axis=-1)
    return jnp.concatenate([t1 * cos - t2 * sin, t1 * sin + t2 * cos], axis=-1).astype(t.dtype)


def to_blocks(t):
    b, s = t.shape[:2]
    t = t.reshape((b, s // Q_BLOCK, Q_BLOCK) + t.shape[2:])
    return jnp.moveaxis(t, 1, 0)


def from_blocks(t):
    t = jnp.moveaxis(t, 0, 1)
    return t.reshape((t.shape[0], t.shape[1] * t.shape[2]) + t.shape[3:])


def stick_breaking_attention(q, k, v):
    s_len, dh = q.shape[1], q.shape[-1]
    scale = dh ** -0.5
    kpos = jnp.arange(s_len)

    def one_block(args):
        i, qi = args
        qpos = i * Q_BLOCK + jnp.arange(Q_BLOCK)
        z = jnp.einsum('bqhd,bkhd->bhqk', qi, k, preferred_element_type=jnp.float32) * scale
        strict = kpos[None, :] < qpos[:, None]
        log_keep = jnp.where(strict, jax.nn.log_sigmoid(-z), 0.0)
        later = lax.cumsum(log_keep, axis=3, reverse=True) - log_keep
        w = jnp.where(strict, jnp.exp(jax.nn.log_sigmoid(z) + later), 0.0)
        return jnp.einsum('bhqk,bkhd->bqhd', w.astype(v.dtype), v)

    out = lax.map(one_block, (jnp.arange(s_len // Q_BLOCK), to_blocks(q)))
    return from_blocks(out)


def mla_attention(q_nope, q_rope, k_nope, k_rope, v):
    s_len = q_nope.shape[1]
    scale = QK_HEAD_DIM ** -0.5
    kchunk = jnp.arange(s_len) // CHUNK

    def one_block(args):
        i, qn, qr = args
        qchunk = (i * Q_BLOCK + jnp.arange(Q_BLOCK)) // CHUNK
        s = (jnp.einsum('bqhd,bkhd->bhqk', qn, k_nope, preferred_element_type=jnp.float32)
             + jnp.einsum('bqhr,bkr->bhqk', qr, k_rope, preferred_element_type=jnp.float32)) * scale
        s = jnp.where(kchunk[None, :] <= qchunk[:, None], s, NEG_INF)
        prob = jax.nn.softmax(s, axis=-1)
        return jnp.einsum('bhqk,bkhd->bqhd', prob.astype(v.dtype), v)

    out = lax.map(one_block, (jnp.arange(s_len // Q_BLOCK), to_blocks(q_nope), to_blocks(q_rope)))
    return from_blocks(out)


def mixer_a(h, w_in, w_out):
    b, s, _ = h.shape
    q, k, v, g = jnp.split(h @ w_in, 4, axis=-1)
    heads = lambda t: t.reshape(b, s, SB_HEADS, SB_HEAD_DIM)
    o = stick_breaking_attention(heads(q), heads(k), heads(v)).reshape(b, s, SB_WIDTH)
    return (o * jax.nn.silu(g)) @ w_out


def shared_kv(x, kv_norm, w_dkv, kv_latent_norm, w_uk, w_uv, cos, sin):
    b, s, _ = x.shape
    ckv, k_rope = jnp.split(rms_norm(x, kv_norm) @ w_dkv, [KV_LORA_RANK], axis=-1)
    ckv = rms_norm(ckv, kv_latent_norm)
    k_nope = (ckv @ w_uk).reshape(b, s, MLA_HEADS, QK_NOPE_DIM)
    v = (ckv @ w_uv).reshape(b, s, MLA_HEADS, V_HEAD_DIM)
    k_rope = apply_rope(k_rope, cos, sin)
    return k_nope, k_rope, v


def mixer_b(h, kv, w_in, q_latent_norm, w_uq, w_out, cos, sin):
    b, s, _ = h.shape
    k_nope, k_rope, v = kv
    cq, g = jnp.split(h @ w_in, [Q_LORA_RANK], axis=-1)
    q = (rms_norm(cq, q_latent_norm) @ w_uq).reshape(b, s, MLA_HEADS, QK_HEAD_DIM)
    q_nope, q_rope = jnp.split(q, [QK_NOPE_DIM], axis=-1)
    q_rope = apply_rope(q_rope, cos[:, None, :], sin[:, None, :])
    o = mla_attention(q_nope, q_rope, k_nope, k_rope, v).reshape(b, s, MLA_WIDTH)
    return (o * jax.nn.silu(g)) @ w_out


def residual_update(x, out, g_post, p_i, w_pp, w_pg):
    x = x + rms_norm(out, g_post)
    return x + (p_i @ w_pp) * jax.nn.sigmoid(x @ w_pg)


def setup_inputs(seed: int = 0) -> dict:
    key = jax.random.key(seed)
    ks = jax.random.split(key, 17)

    def w(k, shape, fan_in):
        return jax.random.normal(k, shape, jnp.float32) * fan_in ** -0.5

    def gain(k, shape):
        return 1.0 + 0.05 * jax.random.normal(k, shape, jnp.float32)

    return {
        "x": jax.random.normal(ks[0], (BATCH, SEQ, D_MODEL), jnp.float32),
        "p": jax.random.normal(ks[1], (DEPTH, BATCH, SEQ, PLE_DIM), jnp.float32),
        "norm_pre": gain(ks[2], (DEPTH, D_MODEL)),
        "norm_post": gain(ks[3], (DEPTH, D_MODEL)),
        "w_in_a": w(ks[4], (N_A, D_MODEL, 4 * SB_WIDTH), D_MODEL),
        "w_out_a": w(ks[5], (N_A, SB_WIDTH, D_MODEL), SB_WIDTH),
        "w_in_b": w(ks[6], (N_B, D_MODEL, Q_LORA_RANK + MLA_WIDTH), D_MODEL),
        "q_latent_norm": gain(ks[7], (N_B, Q_LORA_RANK)),
        "w_uq": w(ks[8], (N_B, Q_LORA_RANK, MLA_HEADS * QK_HEAD_DIM), Q_LORA_RANK),
        "w_out_b": w(ks[9], (N_B, MLA_WIDTH, D_MODEL), MLA_WIDTH),
        "kv_norm": gain(ks[10], (D_MODEL,)),
        "w_dkv": w(ks[11], (D_MODEL, KV_LORA_RANK + QK_ROPE_DIM), D_MODEL),
        "kv_latent_norm": gain(ks[12], (KV_LORA_RANK,)),
        "w_uk": w(ks[13], (KV_LORA_RANK, MLA_HEADS * QK_NOPE_DIM), KV_LORA_RANK),
        "w_uv": w(ks[14], (KV_LORA_RANK, MLA_WIDTH), KV_LORA_RANK),
        "w_ple_proj": w(ks[15], (DEPTH, PLE_DIM, D_MODEL), PLE_DIM),
        "w_ple_gate": w(ks[16], (DEPTH, D_MODEL, D_MODEL), D_MODEL),
    }


def reference(x, p, norm_pre, norm_post, w_in_a, w_out_a, w_in_b, q_latent_norm, w_uq, w_out_b,
              kv_norm, w_dkv, kv_latent_norm, w_uk, w_uv, w_ple_proj, w_ple_gate):
    cos, sin = rope_tables(x.shape[1])
    for i in range(N_A):
        out = mixer_a(rms_norm(x, norm_pre[i]), w_in_a[i], w_out_a[i])
        x = residual_update(x, out, norm_post[i], p[i], w_ple_proj[i], w_ple_gate[i])
    kv = shared_kv(x, kv_norm, w_dkv, kv_latent_norm, w_uk, w_uv, cos, sin)
    for j in range(N_B):
        i = N_A + j
        out = mixer_b(rms_norm(x, norm_pre[i]), kv, w_in_b[j], q_latent_norm[j], w_uq[j], w_out_b[j], cos, sin)
        x = residual_update(x, out, norm_post[i], p[i], w_ple_proj[i], w_ple_gate[i])
    return x
```

```python
import functools

import jax
import jax.numpy as jnp
from jax import lax
from jax.experimental import pallas as pl
from jax.experimental.pallas import tpu as pltpu

RMS_EPS = 1e-6
NEG_INF = -1e30
HEAD_DIM = 128
ROPE_DIM = 64
QK_DIM = HEAD_DIM + ROPE_DIM
CHUNK = 64
ROPE_THETA = 10000.0

V7X_VMEM_LIMIT_BYTES = 56 * 1024 * 1024
BF16 = jnp.bfloat16
F32 = jnp.float32


def _tile(n, pref):
    t = min(n, pref)
    while n % t:
        t //= 2
    return t


def _params(*sem):
    return pltpu.CompilerParams(dimension_semantics=sem,
                                vmem_limit_bytes=V7X_VMEM_LIMIT_BYTES)


def _resident(shape):
    nd = len(shape)
    return pl.BlockSpec(shape, lambda *_: (0,) * nd, pipeline_mode=pl.Buffered(1))


def _rms_scale(v):
    return lax.rsqrt(jnp.mean(v * v, axis=-1, keepdims=True) + RMS_EPS)


def _sigmoid(v):
    return 1.0 / (1.0 + jnp.exp(-v))


def _dot(a, b):
    return jnp.dot(a, b, preferred_element_type=F32)


def _dot_nt(a, b):
    return lax.dot_general(a, b, (((1,), (1,)), ((), ())), preferred_element_type=F32)


def _proj_a_kernel(x_ref, g_ref, w_ref, o_ref, h_scr, *, q_blocks, q_scale):
    j = pl.program_id(1)

    @pl.when(j == 0)
    def _():
        x = x_ref[...]
        h_scr[...] = (x * _rms_scale(x) * g_ref[...]).astype(BF16)

    acc = _dot(h_scr[...], w_ref[...])

    @pl.when(j < q_blocks)
    def _():
        o_ref[...] = (acc * q_scale).astype(BF16)

    @pl.when((j >= q_blocks) & (j < 3 * q_blocks))
    def _():
        o_ref[...] = acc.astype(BF16)

    @pl.when(j >= 3 * q_blocks)
    def _():
        o_ref[...] = (acc * _sigmoid(acc)).astype(BF16)


def _proj_a(x2d, gain, w_bf16):
    t, d = x2d.shape
    n = w_bf16.shape[1]
    width = n // 4
    tm = _tile(t, 1024)
    tn = _tile(width, 1024)
    return pl.pallas_call(
        functools.partial(_proj_a_kernel, q_blocks=width // tn, q_scale=HEAD_DIM ** -0.5),
        out_shape=jax.ShapeDtypeStruct((t, n), BF16),
        grid=(t // tm, n // tn),
        in_specs=[
            pl.BlockSpec((tm, d), lambda i, j: (i, 0)),
            pl.BlockSpec((1, d), lambda i, j: (0, 0)),
            pl.BlockSpec((d, tn), lambda i, j: (0, j)),
        ],
        out_specs=pl.BlockSpec((tm, tn), lambda i, j: (i, j)),
        scratch_shapes=[pltpu.VMEM((tm, d), BF16)],
        compiler_params=_params("parallel", "arbitrary"),
        name="proj_a",
    )(x2d, gain.reshape(1, d), w_bf16)


def _sb_attn_kernel(q_ref, k_ref, v_ref, sg_ref, tri_ref, o_ref, acc_scr, carry_scr, *, tq, tk):
    i = pl.program_id(2)
    ratio = tq // tk
    q = q_ref[...]
    tri = tri_ref[...]
    acc_scr[...] = jnp.zeros_like(acc_scr)
    carry_scr[...] = jnp.zeros_like(carry_scr)

    def block(j, strict):
        start = pl.multiple_of(j * tk, tk)
        kj = k_ref[pl.ds(start, tk), :]
        vj = v_ref[pl.ds(start, tk), :]
        z = _dot_nt(q, kj)
        soft = jnp.log(1.0 + jnp.exp(-jnp.abs(z)))
        log_beta = jnp.minimum(z, 0.0) - soft
        log_keep = log_beta - z
        if strict is not None:
            log_keep = jnp.where(strict, log_keep, 0.0)
        hi = log_keep.astype(BF16)
        lo = (log_keep - hi.astype(F32)).astype(BF16)
        later = _dot(hi, tri) + _dot(lo, tri)
        carry = carry_scr[...]
        w = jnp.exp(log_beta + later + carry)
        if strict is not None:
            w = jnp.where(strict, w, 0.0)
        acc_scr[...] += _dot(w.astype(BF16), vj)
        carry_scr[...] = carry + later[:, 0:1] + log_keep[:, 0:1]

    row = lax.broadcasted_iota(jnp.int32, (tq, tk), 0)
    col = lax.broadcasted_iota(jnp.int32, (tq, tk), 1)
    for d in reversed(range(ratio)):
        block(i * ratio + d, (col + d * tk) < row)

    def body(step, _):
        block(i * ratio - 1 - step, None)
        return 0

    lax.fori_loop(0, i * ratio, body, 0)
    o_ref[...] = (acc_scr[...] * sg_ref[...].astype(F32)).astype(BF16)


def _sb_attention(qkvg, heads):
    b, s, _ = qkvg.shape
    tq = _tile(s, 512)
    tk = _tile(s, 256)
    tri = (lax.broadcasted_iota(jnp.int32, (tk, tk), 0)
           > lax.broadcasted_iota(jnp.int32, (tk, tk), 1)).astype(BF16)
    return pl.pallas_call(
        functools.partial(_sb_attn_kernel, tq=tq, tk=tk),
        out_shape=jax.ShapeDtypeStruct((b, s, heads * HEAD_DIM), BF16),
        grid=(b, heads, s // tq),
        in_specs=[
            pl.BlockSpec((None, tq, HEAD_DIM), lambda bi, h, i: (bi, i, h)),
            pl.BlockSpec((None, s, HEAD_DIM), lambda bi, h, i: (bi, 0, heads + h)),
            pl.BlockSpec((None, s, HEAD_DIM), lambda bi, h, i: (bi, 0, 2 * heads + h)),
            pl.BlockSpec((None, tq, HEAD_DIM), lambda bi, h, i: (bi, i, 3 * heads + h)),
            pl.BlockSpec((tk, tk), lambda bi, h, i: (0, 0)),
        ],
        out_specs=pl.BlockSpec((None, tq, HEAD_DIM), lambda bi, h, i: (bi, i, h)),
        scratch_shapes=[pltpu.VMEM((tq, HEAD_DIM), F32), pltpu.VMEM((tq, 1), F32)],
        compiler_params=_params("parallel", "parallel", "arbitrary"),
        name="sb_attn",
    )(qkvg, qkvg, qkvg, qkvg, tri)


def _out_kernel(og_ref, x_ref, p_ref, wout_ref, gpost_ref, wpp_ref, wpg_ref, gnext_ref,
                xo_ref, *next_refs):
    y = _dot(og_ref[...], wout_ref[...])
    x1 = x_ref[...] + y * _rms_scale(y) * gpost_ref[...]
    gate = _dot(x1.astype(BF16), wpg_ref[...])
    emb = _dot(p_ref[...].astype(BF16), wpp_ref[...])
    x2 = x1 + emb * _sigmoid(gate)
    xo_ref[...] = x2
    if next_refs:
        normed = x2 * _rms_scale(x2)
        for n, ref in enumerate(next_refs):
            ref[...] = (normed * gnext_ref[n:n + 1, :]).astype(BF16)


def _out_proj(og2d, x2d, p2d, w_out, g_post, w_pp, w_pg, next_gains):
    t, d = x2d.shape
    width = og2d.shape[1]
    ple = p2d.shape[1]
    tm = _tile(t, 512)
    n_next = len(next_gains)
    gnext = jnp.stack(next_gains) if n_next else jnp.zeros((1, d), F32)
    row = lambda i: (i, 0)
    outs = pl.pallas_call(
        _out_kernel,
        out_shape=[jax.ShapeDtypeStruct((t, d), F32)]
        + [jax.ShapeDtypeStruct((t, d), BF16)] * n_next,
        grid=(t // tm,),
        in_specs=[
            pl.BlockSpec((tm, width), row),
            pl.BlockSpec((tm, d), row),
            pl.BlockSpec((tm, ple), row),
            _resident((width, d)),
            _resident((1, d)),
            _resident((ple, d)),
            _resident((d, d)),
            _resident(gnext.shape),
        ],
        out_specs=[pl.BlockSpec((tm, d), row)] * (1 + n_next),
        compiler_params=_params("parallel"),
        name="out_proj",
    )(og2d, x2d, p2d, w_out, g_post.reshape(1, d), w_pp, w_pg, gnext)
    return outs[0], list(outs[1:])


def _kv_kernel(h_ref, wd_ref, gl_ref, wuk_ref, wuv_ref, cos_ref, sin_ref, k_ref, v_ref,
               *, rank, heads):
    c = _dot(h_ref[...], wd_ref[...])
    ckv = c[:, :rank]
    k_rope = c[:, rank:rank + ROPE_DIM]
    k_rot = c[:, rank + 128:rank + 128 + ROPE_DIM]
    latent = (ckv * _rms_scale(ckv) * gl_ref[...]).astype(BF16)
    k_nope = _dot(latent, wuk_ref[...]).astype(BF16)
    v_ref[...] = _dot(latent, wuv_ref[...]).astype(BF16)
    cos = cos_ref[...][:, :ROPE_DIM]
    sin = sin_ref[...][:, :ROPE_DIM]
    roped = (k_rope * cos + k_rot * sin).astype(BF16)
    for h in range(heads):
        k_ref[h, :, 0:HEAD_DIM] = k_nope[:, h * HEAD_DIM:(h + 1) * HEAD_DIM]
        k_ref[h, :, HEAD_DIM:QK_DIM] = roped


def _kv_side(hkv, wd, g_latent, w_uk, w_uv, cos, sin, heads):
    b, s, d = hkv.shape
    rank = w_uk.shape[0]
    tm = _tile(s, 512)
    row = lambda bi, i: (bi, i, 0)
    return pl.pallas_call(
        functools.partial(_kv_kernel, rank=rank, heads=heads),
        out_shape=[jax.ShapeDtypeStruct((b, heads, s, QK_DIM), BF16),
                   jax.ShapeDtypeStruct((b, s, heads * HEAD_DIM), BF16)],
        grid=(b, s // tm),
        in_specs=[
            pl.BlockSpec((None, tm, d), row),
            _resident(wd.shape),
            _resident((1, rank)),
            _resident(w_uk.shape),
            _resident(w_uv.shape),
            pl.BlockSpec((tm, 128), lambda bi, i: (i, 0)),
            pl.BlockSpec((tm, 128), lambda bi, i: (i, 0)),
        ],
        out_specs=[pl.BlockSpec((None, heads, tm, QK_DIM), lambda bi, i: (bi, 0, i, 0)),
                   pl.BlockSpec((None, tm, heads * HEAD_DIM), row)],
        compiler_params=_params("parallel", "parallel"),
        name="kv_side",
    )(hkv, wd, g_latent.reshape(1, rank), w_uk, w_uv, cos, sin)


def _q_kernel(h_ref, win_ref, gq_ref, wn_ref, wr_ref, wrot_ref, cos_ref, sin_ref,
              q_ref, sg_ref, *, rank, heads, scale):
    c = _dot(h_ref[...], win_ref[...])
    cq = c[:, :rank]
    gate = c[:, rank:]
    sg_ref[...] = (gate * _sigmoid(gate)).astype(BF16)
    latent = (cq * _rms_scale(cq) * gq_ref[...]).astype(BF16)
    q_nope = (_dot(latent, wn_ref[...]) * scale).astype(BF16)
    reps = heads * ROPE_DIM // 128
    cos = jnp.tile(cos_ref[...], (1, reps))
    sin = jnp.tile(sin_ref[...], (1, reps))
    roped = ((_dot(latent, wr_ref[...]) * cos + _dot(latent, wrot_ref[...]) * sin)
             * scale).astype(BF16)
    for h in range(heads):
        q_ref[h, :, 0:HEAD_DIM] = q_nope[:, h * HEAD_DIM:(h + 1) * HEAD_DIM]
        q_ref[h, :, HEAD_DIM:QK_DIM] = roped[:, h * ROPE_DIM:(h + 1) * ROPE_DIM]


def _q_side(h, w_in, g_latent, w_n, w_r, w_rot, cos, sin, heads):
    b, s, d = h.shape
    rank = w_n.shape[0]
    width = w_in.shape[1] - rank
    tm = _tile(s, 512)
    row = lambda bi, i: (bi, i, 0)
    return pl.pallas_call(
        functools.partial(_q_kernel, rank=rank, heads=heads, scale=QK_DIM ** -0.5),
        out_shape=[jax.ShapeDtypeStruct((b, heads, s, QK_DIM), BF16),
                   jax.ShapeDtypeStruct((b, s, width), BF16)],
        grid=(b, s // tm),
        in_specs=[
            pl.BlockSpec((None, tm, d), row),
            _resident(w_in.shape),
            _resident((1, rank)),
            _resident(w_n.shape),
            _resident(w_r.shape),
            _resident(w_rot.shape),
            pl.BlockSpec((tm, 128), lambda bi, i: (i, 0)),
            pl.BlockSpec((tm, 128), lambda bi, i: (i, 0)),
        ],
        out_specs=[pl.BlockSpec((None, heads, tm, QK_DIM), lambda bi, i: (bi, 0, i, 0)),
                   pl.BlockSpec((None, tm, width), row)],
        compiler_params=_params("parallel", "parallel"),
        name="q_side",
    )(h, w_in, g_latent.reshape(1, rank), w_n, w_r, w_rot, cos, sin)


def _mla_attn_kernel(q_ref, k_ref, v_ref, sg_ref, o_ref, acc_scr, m_scr, l_scr, *, tq, tk):
    i = pl.program_id(2)
    ratio = tq // tk
    q = q_ref[...]
    acc_scr[...] = jnp.zeros_like(acc_scr)
    m_scr[...] = jnp.full_like(m_scr, NEG_INF)
    l_scr[...] = jnp.zeros_like(l_scr)

    def block(j, visible):
        start = pl.multiple_of(j * tk, tk)
        kj = k_ref[pl.ds(start, tk), :]
        vj = v_ref[pl.ds(start, tk), :]
        s = _dot_nt(q, kj)
        if visible is not None:
            s = jnp.where(visible, s, NEG_INF)
        m_prev = m_scr[...]
        m_new = jnp.maximum(m_prev, jnp.max(s, axis=-1, keepdims=True))
        alpha = jnp.exp(m_prev - m_new)
        prob = jnp.exp(s - m_new)
        l_scr[...] = alpha * l_scr[...] + jnp.sum(prob, axis=-1, keepdims=True)
        acc_scr[...] = alpha * acc_scr[...] + _dot(prob.astype(BF16), vj)
        m_scr[...] = m_new

    def body(j, _):
        block(j, None)
        return 0

    lax.fori_loop(0, i * ratio, body, 0)
    qchunk = lax.broadcasted_iota(jnp.int32, (tq, tk), 0) // CHUNK
    kchunk = lax.broadcasted_iota(jnp.int32, (tq, tk), 1) // CHUNK
    for d in range(ratio):
        block(i * ratio + d, (kchunk + d * (tk // CHUNK)) <= qchunk)
    o_ref[...] = (acc_scr[...] / l_scr[...] * sg_ref[...].astype(F32)).astype(BF16)


def _mla_attention(q_cat, k_cat, v, sg):
    b, heads, s, _ = q_cat.shape
    tq = _tile(s, 512)
    tk = _tile(s, 256)
    return pl.pallas_call(
        functools.partial(_mla_attn_kernel, tq=tq, tk=tk),
        out_shape=jax.ShapeDtypeStruct((b, s, heads * HEAD_DIM), BF16),
        grid=(b, heads, s // tq),
        in_specs=[
            pl.BlockSpec((None, None, tq, QK_DIM), lambda bi, h, i: (bi, h, i, 0)),
            pl.BlockSpec((None, None, s, QK_DIM), lambda bi, h, i: (bi, h, 0, 0)),
            pl.BlockSpec((None, s, HEAD_DIM), lambda bi, h, i: (bi, 0, h)),
            pl.BlockSpec((None, tq, HEAD_DIM), lambda bi, h, i: (bi, i, h)),
        ],
        out_specs=pl.BlockSpec((None, tq, HEAD_DIM), lambda bi, h, i: (bi, i, h)),
        scratch_shapes=[pltpu.VMEM((tq, HEAD_DIM), F32), pltpu.VMEM((tq, 1), F32),
                        pltpu.VMEM((tq, 1), F32)],
        compiler_params=_params("parallel", "parallel", "arbitrary"),
        name="mla_attn",
    )(q_cat, k_cat, v, sg)


def _rotate_half_columns(w):
    half = ROPE_DIM // 2
    return jnp.concatenate([-w[..., half:], w[..., :half]], axis=-1)


def _rope_tables(s):
    inv = 1.0 / (ROPE_THETA ** (jnp.arange(0, ROPE_DIM, 2, dtype=F32) / ROPE_DIM))
    ang = jnp.arange(s, dtype=F32)[:, None] * inv[None, :]
    cos, sin = jnp.cos(ang), jnp.sin(ang)
    return jnp.tile(cos, (1, 4)), jnp.tile(sin, (1, 4))


def kernel(x, p, norm_pre, norm_post, w_in_a, w_out_a, w_in_b, q_latent_norm, w_uq, w_out_b,
           kv_norm, w_dkv, kv_latent_norm, w_uk, w_uv, w_ple_proj, w_ple_gate):
    b, s, d = x.shape
    t = b * s
    n_a, n_b = w_in_a.shape[0], w_in_b.shape[0]
    heads = d // HEAD_DIM
    q_rank = w_uq.shape[1]
    kv_rank = w_uk.shape[0]
    cos, sin = _rope_tables(s)

    x2d = x.reshape(t, d)
    p2d = p.reshape(p.shape[0], t, p.shape[-1])

    def finish_layer(layer, og, x2d, w_out):
        if layer + 1 < n_a:
            gains = [norm_pre[layer + 1]]
        elif layer + 1 == n_a and n_b:
            gains = [norm_pre[layer + 1], kv_norm]
        elif layer + 1 < n_a + n_b:
            gains = [norm_pre[layer + 1]]
        else:
            gains = []
        return _out_proj(og.reshape(t, -1), x2d, p2d[layer], w_out.astype(BF16),
                         norm_post[layer], w_ple_proj[layer].astype(BF16),
                         w_ple_gate[layer].astype(BF16), gains)

    h_next = None
    for i in range(n_a):
        if i == 0:
            qkvg = _proj_a(x2d, norm_pre[0], w_in_a[0].astype(BF16))
        else:
            qkvg = _proj_a(x2d, norm_pre[i], w_in_a[i].astype(BF16))
        og = _sb_attention(qkvg.reshape(b, s, -1), heads)
        x2d, h_next = finish_layer(i, og, x2d, w_out_a[i])

    if n_b == 0:
        return x2d.reshape(b, s, d)

    if n_a:
        h_b, h_kv = h_next
    else:
        raise NotImplementedError("a trunk without mixer-A layers is not supported")
    wd = jnp.concatenate([
        w_dkv[:, :kv_rank],
        jnp.pad(w_dkv[:, kv_rank:], ((0, 0), (0, 128 - ROPE_DIM))),
        jnp.pad(_rotate_half_columns(w_dkv[:, kv_rank:]), ((0, 0), (0, 128 - ROPE_DIM))),
    ], axis=1).astype(BF16)
    k_cat, v = _kv_side(h_kv.reshape(b, s, d), wd, kv_latent_norm, w_uk.astype(BF16),
                        w_uv.astype(BF16), cos, sin, heads)

    for j in range(n_b):
        layer = n_a + j
        wq = w_uq[j].reshape(q_rank, heads, QK_DIM)
        w_n = wq[:, :, :HEAD_DIM].reshape(q_rank, heads * HEAD_DIM).astype(BF16)
        w_r = wq[:, :, HEAD_DIM:].reshape(q_rank, heads * ROPE_DIM).astype(BF16)
        w_rot = _rotate_half_columns(wq[:, :, HEAD_DIM:]).reshape(
            q_rank, heads * ROPE_DIM).astype(BF16)
        q_cat, sg = _q_side(h_b.reshape(b, s, d), w_in_b[j].astype(BF16), q_latent_norm[j],
                            w_n, w_r, w_rot, cos, sin, heads)
        og = _mla_attention(q_cat, k_cat, v, sg)
        x2d, h_next = finish_layer(layer, og, x2d, w_out_b[j])
        if h_next:
            h_b = h_next[0]
    return x2d.reshape(b, s, d)
```

```python
import functools
import math

import jax
import jax.numpy as jnp
from jax import lax
from jax.experimental import pallas as pl
from jax.experimental.pallas import tpu as pltpu

RMS_EPS = 1e-6
NEG_INF = -1e30
HEAD_DIM = 128
ROPE_DIM = 64
QK_DIM = HEAD_DIM + ROPE_DIM
CHUNK = 64
ROPE_THETA = 10000.0
LOG2_E = math.log2(math.e)

V7X_VMEM_LIMIT_BYTES = 56 * 1024 * 1024
BF16 = jnp.bfloat16
F32 = jnp.float32


def _tile(n, pref):
    t = min(n, pref)
    while n % t:
        t //= 2
    return t


def _params(*sem):
    return pltpu.CompilerParams(dimension_semantics=sem,
                                vmem_limit_bytes=V7X_VMEM_LIMIT_BYTES)


def _resident(shape):
    nd = len(shape)
    return pl.BlockSpec(shape, lambda *_: (0,) * nd, pipeline_mode=pl.Buffered(1))


def _rms_scale(v):
    return lax.rsqrt(jnp.mean(v * v, axis=-1, keepdims=True) + RMS_EPS)


def _sigmoid(v):
    return 1.0 / (1.0 + jnp.exp(-v))


def _dot(a, b):
    return jnp.dot(a, b, preferred_element_type=F32)


def _dot_nt(a, b):
    return lax.dot_general(a, b, (((1,), (1,)), ((), ())), preferred_element_type=F32)


def _proj_a_kernel(x_ref, g_ref, w_ref, o_ref, h_scr, *, q_blocks, q_scale):
    j = pl.program_id(1)

    @pl.when(j == 0)
    def _():
        x = x_ref[...]
        h_scr[...] = (x * _rms_scale(x) * g_ref[...]).astype(BF16)

    acc = _dot(h_scr[...], w_ref[...])

    @pl.when(j < q_blocks)
    def _():
        o_ref[...] = (acc * q_scale).astype(BF16)

    @pl.when((j >= q_blocks) & (j < 3 * q_blocks))
    def _():
        o_ref[...] = acc.astype(BF16)

    @pl.when(j >= 3 * q_blocks)
    def _():
        o_ref[...] = (acc * _sigmoid(acc)).astype(BF16)


def _proj_a(x2d, gain, w_bf16):
    t, d = x2d.shape
    n = w_bf16.shape[1]
    width = n // 4
    tm = _tile(t, 1024)
    tn = _tile(width, 1024)
    return pl.pallas_call(
        functools.partial(_proj_a_kernel, q_blocks=width // tn,
                          q_scale=HEAD_DIM ** -0.5 * LOG2_E),
        out_shape=jax.ShapeDtypeStruct((t, n), BF16),
        grid=(t // tm, n // tn),
        in_specs=[
            pl.BlockSpec((tm, d), lambda i, j: (i, 0)),
            pl.BlockSpec((1, d), lambda i, j: (0, 0)),
            pl.BlockSpec((d, tn), lambda i, j: (0, j)),
        ],
        out_specs=pl.BlockSpec((tm, tn), lambda i, j: (i, j)),
        scratch_shapes=[pltpu.VMEM((tm, d), BF16)],
        compiler_params=_params("parallel", "arbitrary"),
        name="proj_a",
    )(x2d, gain.reshape(1, d), w_bf16)


def _sb_attn_kernel(q_ref, k_ref, v_ref, sg_ref, upper_ref, o_ref, vt_scr, acc_scr, carry_scr,
                    *, tq, tk):
    i = pl.program_id(2)
    nsub = tq // tk

    @pl.when(i == 0)
    def _():
        for c in range(vt_scr.shape[0]):
            vt_scr[c] = v_ref[c * tk:(c + 1) * tk, :].T

    q_t = q_ref[...].T
    upper2 = upper_ref[...]
    acc_scr[...] = jnp.zeros_like(acc_scr)
    carry_scr[...] = jnp.zeros_like(carry_scr)

    def stage(blk, strict):
        start = pl.multiple_of(blk * tk, tk)
        z = _dot(k_ref[pl.ds(start, tk), :], q_t)
        soft = jnp.log2(1.0 + jnp.exp2(-jnp.abs(z)))
        log_beta = jnp.minimum(z, 0.0) - soft
        log_keep = log_beta - z
        if strict is not None:
            log_keep = jnp.where(strict, log_keep, 0.0)
        hi = log_keep.astype(BF16)
        lo = (log_keep - hi.astype(F32)).astype(BF16)
        later = _dot(upper2, jnp.concatenate([hi, lo], axis=0))
        total = later[0:1, :] + log_keep[0:1, :]
        return log_beta + later, total

    def super_block(sb, masks):
        order = list(reversed(range(nsub)))
        staged = [stage(sb * nsub + c, None if masks is None else masks[c]) for c in order]
        carry = carry_scr[...]
        acc = acc_scr[...]
        for c, (pre, total) in zip(order, staged):
            w = jnp.exp2(pre + carry)
            if masks is not None:
                w = jnp.where(masks[c], w, 0.0)
            acc = acc + _dot(vt_scr[sb * nsub + c], w.astype(BF16))
            carry = carry + total
        acc_scr[...] = acc
        carry_scr[...] = carry

    key = lax.broadcasted_iota(jnp.int32, (tk, tq), 0)
    qry = lax.broadcasted_iota(jnp.int32, (tk, tq), 1)
    super_block(i, [(key + c * tk) < qry for c in range(nsub)])

    def body(step, _):
        super_block(i - 1 - step, None)
        return 0

    lax.fori_loop(0, i, body, 0)
    o_ref[...] = (acc_scr[...].T * sg_ref[...].astype(F32)).astype(BF16)


def _sb_attention(qkvg, heads):
    b, s, _ = qkvg.shape
    tq = _tile(s, 512)
    tk = _tile(s, 256)
    upper = (lax.broadcasted_iota(jnp.int32, (tk, tk), 1)
             > lax.broadcasted_iota(jnp.int32, (tk, tk), 0)).astype(BF16)
    upper2 = jnp.concatenate([upper, upper], axis=1)
    return pl.pallas_call(
        functools.partial(_sb_attn_kernel, tq=tq, tk=tk),
        out_shape=jax.ShapeDtypeStruct((b, s, heads * HEAD_DIM), BF16),
        grid=(b, heads, s // tq),
        in_specs=[
            pl.BlockSpec((None, tq, HEAD_DIM), lambda bi, h, i: (bi, i, h)),
            pl.BlockSpec((None, s, HEAD_DIM), lambda bi, h, i: (bi, 0, heads + h)),
            pl.BlockSpec((None, s, HEAD_DIM), lambda bi, h, i: (bi, 0, 2 * heads + h)),
            pl.BlockSpec((None, tq, HEAD_DIM), lambda bi, h, i: (bi, i, 3 * heads + h)),
            pl.BlockSpec((tk, 2 * tk), lambda bi, h, i: (0, 0)),
        ],
        out_specs=pl.BlockSpec((None, tq, HEAD_DIM), lambda bi, h, i: (bi, i, h)),
        scratch_shapes=[pltpu.VMEM((s // tk, HEAD_DIM, tk), BF16),
                        pltpu.VMEM((HEAD_DIM, tq), F32), pltpu.VMEM((1, tq), F32)],
        compiler_params=_params("parallel", "parallel", "arbitrary"),
        name="sb_attn",
    )(qkvg, qkvg, qkvg, qkvg, upper2)


def _out_kernel(og_ref, x_ref, p_ref, wout_ref, gpost_ref, wpp_ref, wpg_ref, gnext_ref,
                xo_ref, *next_refs):
    y = _dot(og_ref[...], wout_ref[...])
    x1 = x_ref[...] + y * _rms_scale(y) * gpost_ref[...]
    gate = _dot(x1.astype(BF16), wpg_ref[...])
    emb = _dot(p_ref[...].astype(BF16), wpp_ref[...])
    x2 = x1 + emb * _sigmoid(gate)
    xo_ref[...] = x2
    if next_refs:
        normed = x2 * _rms_scale(x2)
        for n, ref in enumerate(next_refs):
            ref[...] = (normed * gnext_ref[n:n + 1, :]).astype(BF16)


def _out_proj(og2d, x2d, p2d, w_out, g_post, w_pp, w_pg, next_gains):
    t, d = x2d.shape
    width = og2d.shape[1]
    ple = p2d.shape[1]
    tm = _tile(t, 512)
    n_next = len(next_gains)
    gnext = jnp.stack(next_gains) if n_next else jnp.zeros((1, d), F32)
    row = lambda i: (i, 0)
    outs = pl.pallas_call(
        _out_kernel,
        out_shape=[jax.ShapeDtypeStruct((t, d), F32)]
        + [jax.ShapeDtypeStruct((t, d), BF16)] * n_next,
        grid=(t // tm,),
        in_specs=[
            pl.BlockSpec((tm, width), row),
            pl.BlockSpec((tm, d), row),
            pl.BlockSpec((tm, ple), row),
            _resident((width, d)),
            _resident((1, d)),
            _resident((ple, d)),
            _resident((d, d)),
            _resident(gnext.shape),
        ],
        out_specs=[pl.BlockSpec((tm, d), row)] * (1 + n_next),
        compiler_params=_params("parallel"),
        name="out_proj",
    )(og2d, x2d, p2d, w_out, g_post.reshape(1, d), w_pp, w_pg, gnext)
    return outs[0], list(outs[1:])


def _kv_kernel(h_ref, wd_ref, gl_ref, wuk_ref, wuvt_ref, cos_ref, sin_ref, k_ref, vt_ref,
               *, rank, heads):
    c = _dot(h_ref[...], wd_ref[...])
    ckv = c[:, :rank]
    k_rope = c[:, rank:rank + ROPE_DIM]
    k_rot = c[:, rank + 128:rank + 128 + ROPE_DIM]
    latent = (ckv * _rms_scale(ckv) * gl_ref[...]).astype(BF16)
    k_nope = _dot(latent, wuk_ref[...]).astype(BF16)
    vt_ref[...] = _dot_nt(wuvt_ref[...], latent).astype(BF16)
    roped = (k_rope * cos_ref[...] + k_rot * sin_ref[...]).astype(BF16)
    for h in range(heads):
        k_ref[h, :, 0:HEAD_DIM] = k_nope[:, h * HEAD_DIM:(h + 1) * HEAD_DIM]
        k_ref[h, :, HEAD_DIM:QK_DIM] = roped


def _kv_side(hkv, wd, g_latent, w_uk, w_uv_t, cos, sin, heads):
    b, s, d = hkv.shape
    rank = w_uk.shape[0]
    tm = _tile(s, 512)
    row = lambda bi, i: (bi, i, 0)
    return pl.pallas_call(
        functools.partial(_kv_kernel, rank=rank, heads=heads),
        out_shape=[jax.ShapeDtypeStruct((b, heads, s, QK_DIM), BF16),
                   jax.ShapeDtypeStruct((b, heads * HEAD_DIM, s), BF16)],
        grid=(b, s // tm),
        in_specs=[
            pl.BlockSpec((None, tm, d), row),
            _resident(wd.shape),
            _resident((1, rank)),
            _resident(w_uk.shape),
            _resident(w_uv_t.shape),
            pl.BlockSpec((tm, ROPE_DIM), lambda bi, i: (i, 0)),
            pl.BlockSpec((tm, ROPE_DIM), lambda bi, i: (i, 0)),
        ],
        out_specs=[pl.BlockSpec((None, heads, tm, QK_DIM), lambda bi, i: (bi, 0, i, 0)),
                   pl.BlockSpec((None, heads * HEAD_DIM, tm), lambda bi, i: (bi, 0, i))],
        compiler_params=_params("parallel", "parallel"),
        name="kv_side",
    )(hkv, wd, g_latent.reshape(1, rank), w_uk, w_uv_t, cos, sin)


def _q_kernel(h_ref, win_ref, gq_ref, wnt_ref, wrt_ref, wrott_ref, cost_ref, sint_ref,
              qt_ref, sg_ref, *, rank, heads, scale):
    c = _dot(h_ref[...], win_ref[...])
    cq = c[:, :rank]
    gate = c[:, rank:]
    sg_ref[...] = (gate * _sigmoid(gate)).astype(BF16)
    latent = (cq * _rms_scale(cq) * gq_ref[...]).astype(BF16)
    nope_t = (_dot_nt(wnt_ref[...], latent) * scale).astype(BF16)
    cos_t = jnp.tile(cost_ref[...], (heads, 1))
    sin_t = jnp.tile(sint_ref[...], (heads, 1))
    roped_t = ((_dot_nt(wrt_ref[...], latent) * cos_t + _dot_nt(wrott_ref[...], latent) * sin_t)
               * scale).astype(BF16)
    for h in range(heads):
        qt_ref[h, 0:HEAD_DIM, :] = nope_t[h * HEAD_DIM:(h + 1) * HEAD_DIM, :]
        qt_ref[h, HEAD_DIM:QK_DIM, :] = roped_t[h * ROPE_DIM:(h + 1) * ROPE_DIM, :]


def _q_side(h, w_in, g_latent, w_n_t, w_r_t, w_rot_t, cos_t, sin_t, heads):
    b, s, d = h.shape
    rank = w_n_t.shape[1]
    width = w_in.shape[1] - rank
    tm = _tile(s, 512)
    row = lambda bi, i: (bi, i, 0)
    return pl.pallas_call(
        functools.partial(_q_kernel, rank=rank, heads=heads, scale=QK_DIM ** -0.5 * LOG2_E),
        out_shape=[jax.ShapeDtypeStruct((b, heads, QK_DIM, s), BF16),
                   jax.ShapeDtypeStruct((b, s, width), BF16)],
        grid=(b, s // tm),
        in_specs=[
            pl.BlockSpec((None, tm, d), row),
            _resident(w_in.shape),
            _resident((1, rank)),
            _resident(w_n_t.shape),
            _resident(w_r_t.shape),
            _resident(w_rot_t.shape),
            pl.BlockSpec((ROPE_DIM, tm), lambda bi, i: (0, i)),
            pl.BlockSpec((ROPE_DIM, tm), lambda bi, i: (0, i)),
        ],
        out_specs=[pl.BlockSpec((None, heads, QK_DIM, tm), lambda bi, i: (bi, 0, 0, i)),
                   pl.BlockSpec((None, tm, width), row)],
        compiler_params=_params("parallel", "parallel"),
        name="q_side",
    )(h, w_in, g_latent.reshape(1, rank), w_n_t, w_r_t, w_rot_t, cos_t, sin_t)


def _mla_attn_kernel(qt_ref, k_ref, vt_ref, sg_ref, o_ref, acc_scr, m_scr, l_scr, *, tq, tk):
    i = pl.program_id(2)
    nsub = tq // tk
    q_t = qt_ref[...]
    acc_scr[...] = jnp.zeros_like(acc_scr)
    m_scr[...] = jnp.full_like(m_scr, NEG_INF)
    l_scr[...] = jnp.zeros_like(l_scr)

    def partial_softmax(blk, visible):
        start = pl.multiple_of(blk * tk, tk)
        s = _dot(k_ref[pl.ds(start, tk), :], q_t)
        if visible is not None:
            s = jnp.where(visible, s, NEG_INF)
        m_c = jnp.max(s, axis=0, keepdims=True)
        prob = jnp.exp2(s - m_c)
        l_c = jnp.sum(prob, axis=0, keepdims=True)
        o_c = _dot(vt_ref[:, pl.ds(start, tk)], prob.astype(BF16))
        return m_c, l_c, o_c

    def super_block(sb, masks):
        parts = [partial_softmax(sb * nsub + c, None if masks is None else masks[c])
                 for c in range(nsub)]
        m_prev = m_scr[...]
        m_new = m_prev
        for m_c, _, _ in parts:
            m_new = jnp.maximum(m_new, m_c)
        alpha = jnp.exp2(m_prev - m_new)
        acc = acc_scr[...] * alpha
        l_run = l_scr[...] * alpha
        for m_c, l_c, o_c in parts:
            weight = jnp.exp2(m_c - m_new)
            acc = acc + o_c * weight
            l_run = l_run + l_c * weight
        acc_scr[...] = acc
        l_scr[...] = l_run
        m_scr[...] = m_new

    def body(sb, _):
        super_block(sb, None)
        return 0

    lax.fori_loop(0, i, body, 0)
    kchunk = lax.broadcasted_iota(jnp.int32, (tk, tq), 0) // CHUNK
    qchunk = lax.broadcasted_iota(jnp.int32, (tk, tq), 1) // CHUNK
    super_block(i, [(kchunk + c * (tk // CHUNK)) <= qchunk for c in range(nsub)])
    out = (acc_scr[...] / l_scr[...]).T
    o_ref[...] = (out * sg_ref[...].astype(F32)).astype(BF16)


def _mla_attention(q_t, k_cat, v_t, sg):
    b, heads, _, s = q_t.shape
    tq = _tile(s, 512)
    tk = _tile(s, 256)
    return pl.pallas_call(
        functools.partial(_mla_attn_kernel, tq=tq, tk=tk),
        out_shape=jax.ShapeDtypeStruct((b, s, heads * HEAD_DIM), BF16),
        grid=(b, heads, s // tq),
        in_specs=[
            pl.BlockSpec((None, None, QK_DIM, tq), lambda bi, h, i: (bi, h, 0, i)),
            pl.BlockSpec((None, None, s, QK_DIM), lambda bi, h, i: (bi, h, 0, 0)),
            pl.BlockSpec((None, HEAD_DIM, s), lambda bi, h, i: (bi, h, 0)),
            pl.BlockSpec((None, tq, HEAD_DIM), lambda bi, h, i: (bi, i, h)),
        ],
        out_specs=pl.BlockSpec((None, tq, HEAD_DIM), lambda bi, h, i: (bi, i, h)),
        scratch_shapes=[pltpu.VMEM((HEAD_DIM, tq), F32), pltpu.VMEM((1, tq), F32),
                        pltpu.VMEM((1, tq), F32)],
        compiler_params=_params("parallel", "parallel", "arbitrary"),
        name="mla_attn",
    )(q_t, k_cat, v_t, sg)


def _rotate_half_columns(w):
    half = ROPE_DIM // 2
    return jnp.concatenate([-w[..., half:], w[..., :half]], axis=-1)


def _rope_tables(s):
    inv = 1.0 / (ROPE_THETA ** (jnp.arange(0, ROPE_DIM, 2, dtype=F32) / ROPE_DIM))
    ang = jnp.arange(s, dtype=F32)[:, None] * inv[None, :]
    cos, sin = jnp.cos(ang), jnp.sin(ang)
    return jnp.tile(cos, (1, 2)), jnp.tile(sin, (1, 2))


def kernel(x, p, norm_pre, norm_post, w_in_a, w_out_a, w_in_b, q_latent_norm, w_uq, w_out_b,
           kv_norm, w_dkv, kv_latent_norm, w_uk, w_uv, w_ple_proj, w_ple_gate):
    b, s, d = x.shape
    t = b * s
    n_a, n_b = w_in_a.shape[0], w_in_b.shape[0]
    heads = d // HEAD_DIM
    q_rank = w_uq.shape[1]
    kv_rank = w_uk.shape[0]
    cos, sin = _rope_tables(s)

    x2d = x.reshape(t, d)
    p2d = p.reshape(p.shape[0], t, p.shape[-1])

    def finish_layer(layer, og, x2d, w_out):
        if layer + 1 < n_a:
            gains = [norm_pre[layer + 1]]
        elif layer + 1 == n_a and n_b:
            gains = [norm_pre[layer + 1], kv_norm]
        elif layer + 1 < n_a + n_b:
            gains = [norm_pre[layer + 1]]
        else:
            gains = []
        return _out_proj(og.reshape(t, -1), x2d, p2d[layer], w_out.astype(BF16),
                         norm_post[layer], w_ple_proj[layer].astype(BF16),
                         w_ple_gate[layer].astype(BF16), gains)

    h_next = None
    for i in range(n_a):
        qkvg = _proj_a(x2d, norm_pre[i], w_in_a[i].astype(BF16))
        og = _sb_attention(qkvg.reshape(b, s, -1), heads)
        x2d, h_next = finish_layer(i, og, x2d, w_out_a[i])

    if n_b == 0:
        return x2d.reshape(b, s, d)
    if n_a == 0:
        raise NotImplementedError("a trunk without mixer-A layers is not supported")

    h_b, h_kv = h_next
    wd = jnp.concatenate([
        w_dkv[:, :kv_rank],
        jnp.pad(w_dkv[:, kv_rank:], ((0, 0), (0, 128 - ROPE_DIM))),
        jnp.pad(_rotate_half_columns(w_dkv[:, kv_rank:]), ((0, 0), (0, 128 - ROPE_DIM))),
    ], axis=1).astype(BF16)
    k_cat, v_t = _kv_side(h_kv.reshape(b, s, d), wd, kv_latent_norm, w_uk.astype(BF16),
                          w_uv.T.astype(BF16), cos, sin, heads)

    for j in range(n_b):
        layer = n_a + j
        wq = w_uq[j].reshape(q_rank, heads, QK_DIM)
        w_n_t = wq[:, :, :HEAD_DIM].reshape(q_rank, heads * HEAD_DIM).T.astype(BF16)
        w_r_t = wq[:, :, HEAD_DIM:].reshape(q_rank, heads * ROPE_DIM).T.astype(BF16)
        w_rot_t = _rotate_half_columns(wq[:, :, HEAD_DIM:]).reshape(
            q_rank, heads * ROPE_DIM).T.astype(BF16)
        q_t, sg = _q_side(h_b.reshape(b, s, d), w_in_b[j].astype(BF16), q_latent_norm[j],
                          w_n_t, w_r_t, w_rot_t, cos.T, sin.T, heads)
        og = _mla_attention(q_t, k_cat, v_t, sg)
        x2d, h_next = finish_layer(layer, og, x2d, w_out_b[j])
        if h_next:
            h_b = h_next[0]
    return x2d.reshape(b, s, d)
```

```python
import functools
import math

import jax
import jax.numpy as jnp
from jax import lax
from jax.experimental import pallas as pl
from jax.experimental.pallas import tpu as pltpu

RMS_EPS = 1e-6
NEG_INF = -1e30
HEAD_DIM = 128
ROPE_DIM = 64
QK_DIM = HEAD_DIM + ROPE_DIM
CHUNK = 64
ROPE_THETA = 10000.0
LOG2_E = math.log2(math.e)
EXP2_UNDERFLOW = -160.0

V7X_VMEM_LIMIT_BYTES = 56 * 1024 * 1024
BF16 = jnp.bfloat16
F32 = jnp.float32


def _tile(n, pref):
    t = min(n, pref)
    while n % t:
        t //= 2
    return t


def _params(*sem, flags=None):
    return pltpu.CompilerParams(dimension_semantics=sem,
                                vmem_limit_bytes=V7X_VMEM_LIMIT_BYTES, flags=flags)


def _resident(shape):
    nd = len(shape)
    return pl.BlockSpec(shape, lambda *_: (0,) * nd, pipeline_mode=pl.Buffered(1))


def _rms_scale(v):
    return lax.rsqrt(jnp.mean(v * v, axis=-1, keepdims=True) + RMS_EPS)


def _sigmoid(v):
    return 1.0 / (1.0 + jnp.exp(-v))


def _dot(a, b):
    return jnp.dot(a, b, preferred_element_type=F32)


def _dot_nt(a, b):
    return lax.dot_general(a, b, (((1,), (1,)), ((), ())), preferred_element_type=F32)


def _proj_a_kernel(x_ref, g_ref, w_ref, o_ref, h_scr, *, q_blocks, q_scale):
    j = pl.program_id(1)

    @pl.when(j == 0)
    def _():
        x = x_ref[...]
        h_scr[...] = (x * _rms_scale(x) * g_ref[...]).astype(BF16)

    acc = _dot(h_scr[...], w_ref[...])

    @pl.when(j < q_blocks)
    def _():
        o_ref[...] = (acc * q_scale).astype(BF16)

    @pl.when((j >= q_blocks) & (j < 3 * q_blocks))
    def _():
        o_ref[...] = acc.astype(BF16)

    @pl.when(j >= 3 * q_blocks)
    def _():
        o_ref[...] = (acc * _sigmoid(acc)).astype(BF16)


def _proj_a(x2d, gain, w_bf16):
    t, d = x2d.shape
    n = w_bf16.shape[1]
    width = n // 4
    tm = _tile(t, 1024)
    tn = _tile(width, 1024)
    return pl.pallas_call(
        functools.partial(_proj_a_kernel, q_blocks=width // tn,
                          q_scale=HEAD_DIM ** -0.5 * LOG2_E),
        out_shape=jax.ShapeDtypeStruct((t, n), BF16),
        grid=(t // tm, n // tn),
        in_specs=[
            pl.BlockSpec((tm, d), lambda i, j: (i, 0)),
            pl.BlockSpec((1, d), lambda i, j: (0, 0)),
            pl.BlockSpec((d, tn), lambda i, j: (0, j)),
        ],
        out_specs=pl.BlockSpec((tm, tn), lambda i, j: (i, j)),
        scratch_shapes=[pltpu.VMEM((tm, d), BF16)],
        compiler_params=_params("parallel", "arbitrary"),
        name="proj_a",
    )(x2d, gain.reshape(1, d), w_bf16)


def _sb_attn_kernel(q_ref, k_ref, v_ref, sg_ref, upper_ref, o_ref, vt_scr, acc_scr, carry_scr,
                    *, tq, tk):
    i = pl.program_id(2)
    nsub = tq // tk

    @pl.when(i == 0)
    def _():
        for c in range(vt_scr.shape[0]):
            vt_scr[c] = v_ref[c * tk:(c + 1) * tk, :].T

    q_t = q_ref[...].T
    upper2 = upper_ref[...]
    acc_scr[...] = jnp.zeros_like(acc_scr)
    carry_scr[...] = jnp.zeros_like(carry_scr)

    def scores(blk):
        start = pl.multiple_of(blk * tk, tk)
        return _dot(k_ref[pl.ds(start, tk), :], q_t)

    def stage(z, strict):
        soft = jnp.log2(1.0 + jnp.exp2(-jnp.abs(z)))
        log_beta = jnp.minimum(z, 0.0) - soft
        log_keep = log_beta - z
        if strict is not None:
            log_keep = jnp.where(strict, log_keep, 0.0)
        hi = log_keep.astype(BF16)
        lo = (log_keep - hi.astype(F32)).astype(BF16)
        later = _dot(upper2, jnp.concatenate([hi, lo], axis=0))
        total = later[0:1, :] + log_keep[0:1, :]
        return log_beta + later, total

    def super_block(sb, masks):
        order = list(reversed(range(nsub)))
        raw = [scores(sb * nsub + c) for c in order]
        staged = [stage(z, None if masks is None else masks[c]) for c, z in zip(order, raw)]
        carry = carry_scr[...]
        acc = acc_scr[...]
        for c, (pre, total) in zip(order, staged):
            w = jnp.exp2(pre + carry)
            if masks is not None:
                w = jnp.where(masks[c], w, 0.0)
            acc = acc + _dot(vt_scr[sb * nsub + c], w.astype(BF16))
            carry = carry + total
        acc_scr[...] = acc
        carry_scr[...] = carry

    key = lax.broadcasted_iota(jnp.int32, (tk, tq), 0)
    qry = lax.broadcasted_iota(jnp.int32, (tk, tq), 1)
    super_block(i, [(key + c * tk) < qry for c in range(nsub)])

    def live(sb):
        return (sb >= 0) & (jnp.max(carry_scr[...]) > EXP2_UNDERFLOW)

    def body(sb):
        super_block(sb, None)
        return sb - 1

    lax.while_loop(live, body, i - 1)
    o_ref[...] = (acc_scr[...].T * sg_ref[...].astype(F32)).astype(BF16)


def _sb_attention(qkvg, heads):
    b, s, _ = qkvg.shape
    tq = _tile(s, 512)
    tk = _tile(s, 256)
    upper = (lax.broadcasted_iota(jnp.int32, (tk, tk), 1)
             > lax.broadcasted_iota(jnp.int32, (tk, tk), 0)).astype(BF16)
    upper2 = jnp.concatenate([upper, upper], axis=1)
    return pl.pallas_call(
        functools.partial(_sb_attn_kernel, tq=tq, tk=tk),
        out_shape=jax.ShapeDtypeStruct((b, s, heads * HEAD_DIM), BF16),
        grid=(b, heads, s // tq),
        in_specs=[
            pl.BlockSpec((None, tq, HEAD_DIM), lambda bi, h, i: (bi, i, h)),
            pl.BlockSpec((None, s, HEAD_DIM), lambda bi, h, i: (bi, 0, heads + h)),
            pl.BlockSpec((None, s, HEAD_DIM), lambda bi, h, i: (bi, 0, 2 * heads + h)),
            pl.BlockSpec((None, tq, HEAD_DIM), lambda bi, h, i: (bi, i, 3 * heads + h)),
            pl.BlockSpec((tk, 2 * tk), lambda bi, h, i: (0, 0)),
        ],
        out_specs=pl.BlockSpec((None, tq, HEAD_DIM), lambda bi, h, i: (bi, i, h)),
        scratch_shapes=[pltpu.VMEM((s // tk, HEAD_DIM, tk), BF16),
                        pltpu.VMEM((HEAD_DIM, tq), F32), pltpu.VMEM((1, tq), F32)],
        compiler_params=_params("parallel", "parallel", "arbitrary"),
        name="sb_attn",
    )(qkvg, qkvg, qkvg, qkvg, upper2)


def _out_kernel(og_ref, x_ref, p_ref, wout_ref, gpost_ref, wpp_ref, wpg_ref, gnext_ref,
                xo_ref, *next_refs):
    y = _dot(og_ref[...], wout_ref[...])
    x1 = x_ref[...] + y * _rms_scale(y) * gpost_ref[...]
    gate = _dot(x1.astype(BF16), wpg_ref[...])
    emb = _dot(p_ref[...].astype(BF16), wpp_ref[...])
    x2 = x1 + emb * _sigmoid(gate)
    xo_ref[...] = x2
    if next_refs:
        normed = x2 * _rms_scale(x2)
        for n, ref in enumerate(next_refs):
            ref[...] = (normed * gnext_ref[n:n + 1, :]).astype(BF16)


def _out_proj(og2d, x2d, p2d, w_out, g_post, w_pp, w_pg, next_gains):
    t, d = x2d.shape
    width = og2d.shape[1]
    ple = p2d.shape[1]
    tm = _tile(t, 512)
    n_next = len(next_gains)
    gnext = jnp.stack(next_gains) if n_next else jnp.zeros((1, d), F32)
    row = lambda i: (i, 0)
    outs = pl.pallas_call(
        _out_kernel,
        out_shape=[jax.ShapeDtypeStruct((t, d), F32)]
        + [jax.ShapeDtypeStruct((t, d), BF16)] * n_next,
        grid=(t // tm,),
        in_specs=[
            pl.BlockSpec((tm, width), row),
            pl.BlockSpec((tm, d), row),
            pl.BlockSpec((tm, ple), row),
            _resident((width, d)),
            _resident((1, d)),
            _resident((ple, d)),
            _resident((d, d)),
            _resident(gnext.shape),
        ],
        out_specs=[pl.BlockSpec((tm, d), row)] * (1 + n_next),
        compiler_params=_params("parallel"),
        name="out_proj",
    )(og2d, x2d, p2d, w_out, g_post.reshape(1, d), w_pp, w_pg, gnext)
    return outs[0], list(outs[1:])


def _kv_kernel(h_ref, wd_ref, gl_ref, wuk_ref, wuvt_ref, cos_ref, sin_ref, k_ref, vt_ref,
               *, rank, heads):
    c = _dot(h_ref[...], wd_ref[...])
    ckv = c[:, :rank]
    k_rope = c[:, rank:rank + ROPE_DIM]
    k_rot = c[:, rank + 128:rank + 128 + ROPE_DIM]
    latent = (ckv * _rms_scale(ckv) * gl_ref[...]).astype(BF16)
    k_nope = _dot(latent, wuk_ref[...]).astype(BF16)
    vt_ref[...] = _dot_nt(wuvt_ref[...], latent).astype(BF16)
    roped = (k_rope * cos_ref[...] + k_rot * sin_ref[...]).astype(BF16)
    for h in range(heads):
        k_ref[h, :, 0:HEAD_DIM] = k_nope[:, h * HEAD_DIM:(h + 1) * HEAD_DIM]
        k_ref[h, :, HEAD_DIM:QK_DIM] = roped


def _kv_side(hkv, wd, g_latent, w_uk, w_uv_t, cos, sin, heads):
    b, s, d = hkv.shape
    rank = w_uk.shape[0]
    tm = _tile(s, 512)
    row = lambda bi, i: (bi, i, 0)
    return pl.pallas_call(
        functools.partial(_kv_kernel, rank=rank, heads=heads),
        out_shape=[jax.ShapeDtypeStruct((b, heads, s, QK_DIM), BF16),
                   jax.ShapeDtypeStruct((b, heads * HEAD_DIM, s), BF16)],
        grid=(b, s // tm),
        in_specs=[
            pl.BlockSpec((None, tm, d), row),
            _resident(wd.shape),
            _resident((1, rank)),
            _resident(w_uk.shape),
            _resident(w_uv_t.shape),
            pl.BlockSpec((tm, ROPE_DIM), lambda bi, i: (i, 0)),
            pl.BlockSpec((tm, ROPE_DIM), lambda bi, i: (i, 0)),
        ],
        out_specs=[pl.BlockSpec((None, heads, tm, QK_DIM), lambda bi, i: (bi, 0, i, 0)),
                   pl.BlockSpec((None, heads * HEAD_DIM, tm), lambda bi, i: (bi, 0, i))],
        compiler_params=_params("parallel", "parallel"),
        name="kv_side",
    )(hkv, wd, g_latent.reshape(1, rank), w_uk, w_uv_t, cos, sin)


def _q_kernel(h_ref, win_ref, gq_ref, wnt_ref, wrt_ref, wrott_ref, cost_ref, sint_ref,
              qt_ref, sg_ref, *, rank, heads, scale):
    c = _dot(h_ref[...], win_ref[...])
    cq = c[:, :rank]
    gate = c[:, rank:]
    sg_ref[...] = (gate * _sigmoid(gate)).astype(BF16)
    latent = (cq * _rms_scale(cq) * gq_ref[...]).astype(BF16)
    nope_t = (_dot_nt(wnt_ref[...], latent) * scale).astype(BF16)
    cos_t = jnp.tile(cost_ref[...], (heads, 1))
    sin_t = jnp.tile(sint_ref[...], (heads, 1))
    roped_t = ((_dot_nt(wrt_ref[...], latent) * cos_t + _dot_nt(wrott_ref[...], latent) * sin_t)
               * scale).astype(BF16)
    for h in range(heads):
        qt_ref[h, 0:HEAD_DIM, :] = nope_t[h * HEAD_DIM:(h + 1) * HEAD_DIM, :]
        qt_ref[h, HEAD_DIM:QK_DIM, :] = roped_t[h * ROPE_DIM:(h + 1) * ROPE_DIM, :]


def _q_side(h, w_in, g_latent, w_n_t, w_r_t, w_rot_t, cos_t, sin_t, heads):
    b, s, d = h.shape
    rank = w_n_t.shape[1]
    width = w_in.shape[1] - rank
    tm = _tile(s, 512)
    row = lambda bi, i: (bi, i, 0)
    return pl.pallas_call(
        functools.partial(_q_kernel, rank=rank, heads=heads, scale=QK_DIM ** -0.5 * LOG2_E),
        out_shape=[jax.ShapeDtypeStruct((b, heads, QK_DIM, s), BF16),
                   jax.ShapeDtypeStruct((b, s, width), BF16)],
        grid=(b, s // tm),
        in_specs=[
            pl.BlockSpec((None, tm, d), row),
            _resident(w_in.shape),
            _resident((1, rank)),
            _resident(w_n_t.shape),
            _resident(w_r_t.shape),
            _resident(w_rot_t.shape),
            pl.BlockSpec((ROPE_DIM, tm), lambda bi, i: (0, i)),
            pl.BlockSpec((ROPE_DIM, tm), lambda bi, i: (0, i)),
        ],
        out_specs=[pl.BlockSpec((None, heads, QK_DIM, tm), lambda bi, i: (bi, 0, 0, i)),
                   pl.BlockSpec((None, tm, width), row)],
        compiler_params=_params("parallel", "parallel"),
        name="q_side",
    )(h, w_in, g_latent.reshape(1, rank), w_n_t, w_r_t, w_rot_t, cos_t, sin_t)


def _mla_attn_kernel(qt_ref, k_ref, vt_ref, sg_ref, o_ref, acc_scr, m_scr, l_scr, *, tq, tk):
    i = pl.program_id(2)
    nsub = tq // tk
    q_t = qt_ref[...]
    acc_scr[...] = jnp.zeros_like(acc_scr)
    m_scr[...] = jnp.full_like(m_scr, NEG_INF)
    l_scr[...] = jnp.zeros_like(l_scr)

    def scores(blk):
        start = pl.multiple_of(blk * tk, tk)
        return _dot(k_ref[pl.ds(start, tk), :], q_t)

    def partial_softmax(blk, s, visible):
        start = pl.multiple_of(blk * tk, tk)
        if visible is not None:
            s = jnp.where(visible, s, NEG_INF)
        m_c = jnp.max(s, axis=0, keepdims=True)
        prob = jnp.exp2(s - m_c)
        l_c = jnp.sum(prob, axis=0, keepdims=True)
        o_c = _dot(vt_ref[:, pl.ds(start, tk)], prob.astype(BF16))
        return m_c, l_c, o_c

    def super_block(sb, masks):
        raw = [scores(sb * nsub + c) for c in range(nsub)]
        parts = [partial_softmax(sb * nsub + c, raw[c], None if masks is None else masks[c])
                 for c in range(nsub)]
        m_prev = m_scr[...]
        m_new = m_prev
        for m_c, _, _ in parts:
            m_new = jnp.maximum(m_new, m_c)
        alpha = jnp.exp2(m_prev - m_new)
        acc = acc_scr[...] * alpha
        l_run = l_scr[...] * alpha
        for m_c, l_c, o_c in parts:
            weight = jnp.exp2(m_c - m_new)
            acc = acc + o_c * weight
            l_run = l_run + l_c * weight
        acc_scr[...] = acc
        l_scr[...] = l_run
        m_scr[...] = m_new

    def body(sb, _):
        super_block(sb, None)
        return 0

    lax.fori_loop(0, i, body, 0)
    kchunk = lax.broadcasted_iota(jnp.int32, (tk, tq), 0) // CHUNK
    qchunk = lax.broadcasted_iota(jnp.int32, (tk, tq), 1) // CHUNK
    super_block(i, [(kchunk + c * (tk // CHUNK)) <= qchunk for c in range(nsub)])
    out = (acc_scr[...] / l_scr[...]).T
    o_ref[...] = (out * sg_ref[...].astype(F32)).astype(BF16)


def _mla_attention(q_t, k_cat, v_t, sg):
    b, heads, _, s = q_t.shape
    tq = _tile(s, 512)
    tk = _tile(s, 256)
    return pl.pallas_call(
        functools.partial(_mla_attn_kernel, tq=tq, tk=tk),
        out_shape=jax.ShapeDtypeStruct((b, s, heads * HEAD_DIM), BF16),
        grid=(b, heads, s // tq),
        in_specs=[
            pl.BlockSpec((None, None, QK_DIM, tq), lambda bi, h, i: (bi, h, 0, i)),
            pl.BlockSpec((None, None, s, QK_DIM), lambda bi, h, i: (bi, h, 0, 0)),
            pl.BlockSpec((None, HEAD_DIM, s), lambda bi, h, i: (bi, h, 0)),
            pl.BlockSpec((None, tq, HEAD_DIM), lambda bi, h, i: (bi, i, h)),
        ],
        out_specs=pl.BlockSpec((None, tq, HEAD_DIM), lambda bi, h, i: (bi, i, h)),
        scratch_shapes=[pltpu.VMEM((HEAD_DIM, tq), F32), pltpu.VMEM((1, tq), F32),
                        pltpu.VMEM((1, tq), F32)],
        compiler_params=_params("parallel", "parallel", "arbitrary"),
        name="mla_attn",
    )(q_t, k_cat, v_t, sg)


def _rotate_half_columns(w):
    half = ROPE_DIM // 2
    return jnp.concatenate([-w[..., half:], w[..., :half]], axis=-1)


def _rope_tables(s):
    inv = 1.0 / (ROPE_THETA ** (jnp.arange(0, ROPE_DIM, 2, dtype=F32) / ROPE_DIM))
    ang = jnp.arange(s, dtype=F32)[:, None] * inv[None, :]
    cos, sin = jnp.cos(ang), jnp.sin(ang)
    return jnp.tile(cos, (1, 2)), jnp.tile(sin, (1, 2))


def kernel(x, p, norm_pre, norm_post, w_in_a, w_out_a, w_in_b, q_latent_norm, w_uq, w_out_b,
           kv_norm, w_dkv, kv_latent_norm, w_uk, w_uv, w_ple_proj, w_ple_gate):
    b, s, d = x.shape
    t = b * s
    n_a, n_b = w_in_a.shape[0], w_in_b.shape[0]
    heads = d // HEAD_DIM
    q_rank = w_uq.shape[1]
    kv_rank = w_uk.shape[0]
    cos, sin = _rope_tables(s)

    x2d = x.reshape(t, d)
    p2d = p.reshape(p.shape[0], t, p.shape[-1])

    def finish_layer(layer, og, x2d, w_out):
        if layer + 1 < n_a:
            gains = [norm_pre[layer + 1]]
        elif layer + 1 == n_a and n_b:
            gains = [norm_pre[layer + 1], kv_norm]
        elif layer + 1 < n_a + n_b:
            gains = [norm_pre[layer + 1]]
        else:
            gains = []
        return _out_proj(og.reshape(t, -1), x2d, p2d[layer], w_out.astype(BF16),
                         norm_post[layer], w_ple_proj[layer].astype(BF16),
                         w_ple_gate[layer].astype(BF16), gains)

    h_next = None
    for i in range(n_a):
        qkvg = _proj_a(x2d, norm_pre[i], w_in_a[i].astype(BF16))
        og = _sb_attention(qkvg.reshape(b, s, -1), heads)
        x2d, h_next = finish_layer(i, og, x2d, w_out_a[i])

    if n_b == 0:
        return x2d.reshape(b, s, d)
    if n_a == 0:
        raise NotImplementedError("a trunk without mixer-A layers is not supported")

    h_b, h_kv = h_next
    wd = jnp.concatenate([
        w_dkv[:, :kv_rank],
        jnp.pad(w_dkv[:, kv_rank:], ((0, 0), (0, 128 - ROPE_DIM))),
        jnp.pad(_rotate_half_columns(w_dkv[:, kv_rank:]), ((0, 0), (0, 128 - ROPE_DIM))),
    ], axis=1).astype(BF16)
    k_cat, v_t = _kv_side(h_kv.reshape(b, s, d), wd, kv_latent_norm, w_uk.astype(BF16),
                          w_uv.T.astype(BF16), cos, sin, heads)

    for j in range(n_b):
        layer = n_a + j
        wq = w_uq[j].reshape(q_rank, heads, QK_DIM)
        w_n_t = wq[:, :, :HEAD_DIM].reshape(q_rank, heads * HEAD_DIM).T.astype(BF16)
        w_r_t = wq[:, :, HEAD_DIM:].reshape(q_rank, heads * ROPE_DIM).T.astype(BF16)
        w_rot_t = _rotate_half_columns(wq[:, :, HEAD_DIM:]).reshape(
            q_rank, heads * ROPE_DIM).T.astype(BF16)
        q_t, sg = _q_side(h_b.reshape(b, s, d), w_in_b[j].astype(BF16), q_latent_norm[j],
                          w_n_t, w_r_t, w_rot_t, cos.T, sin.T, heads)
        og = _mla_attention(q_t, k_cat, v_t, sg)
        x2d, h_next = finish_layer(layer, og, x2d, w_out_b[j])
        if h_next:
            h_b = h_next[0]
    return x2d.reshape(b, s, d)
```

```python
import functools
import math

import jax
import jax.numpy as jnp
from jax import lax
from jax.experimental import pallas as pl
from jax.experimental.pallas import tpu as pltpu

RMS_EPS = 1e-6
NEG_INF = -1e30
HEAD_DIM = 128
ROPE_DIM = 64
QK_DIM = HEAD_DIM + ROPE_DIM
CHUNK = 64
ROPE_THETA = 10000.0
LOG2_E = math.log2(math.e)
EXP2_UNDERFLOW = -160.0

V7X_VMEM_LIMIT_BYTES = 56 * 1024 * 1024
BF16 = jnp.bfloat16
F32 = jnp.float32


def _tile(n, pref):
    t = min(n, pref)
    while n % t:
        t //= 2
    return t


def _params(*sem, flags=None):
    return pltpu.CompilerParams(dimension_semantics=sem,
                                vmem_limit_bytes=V7X_VMEM_LIMIT_BYTES, flags=flags)


def _resident(shape):
    nd = len(shape)
    return pl.BlockSpec(shape, lambda *_: (0,) * nd, pipeline_mode=pl.Buffered(1))


def _rms_scale(v):
    return lax.rsqrt(jnp.mean(v * v, axis=-1, keepdims=True) + RMS_EPS)


def _sigmoid(v):
    return 1.0 / (1.0 + jnp.exp(-v))


def _dot(a, b):
    return jnp.dot(a, b, preferred_element_type=F32)


def _dot_nt(a, b):
    return lax.dot_general(a, b, (((1,), (1,)), ((), ())), preferred_element_type=F32)


def _proj_a_kernel(x_ref, g_ref, w_ref, o_ref, h_scr, *, q_blocks, q_scale):
    j = pl.program_id(1)

    @pl.when(j == 0)
    def _():
        x = x_ref[...]
        h_scr[...] = (x * _rms_scale(x) * g_ref[...]).astype(BF16)

    acc = _dot(h_scr[...], w_ref[...])

    @pl.when(j < q_blocks)
    def _():
        o_ref[...] = (acc * q_scale).astype(BF16)

    @pl.when((j >= q_blocks) & (j < 3 * q_blocks))
    def _():
        o_ref[...] = acc.astype(BF16)

    @pl.when(j >= 3 * q_blocks)
    def _():
        o_ref[...] = (acc * _sigmoid(acc)).astype(BF16)


def _proj_a(x2d, gain, w_bf16):
    t, d = x2d.shape
    n = w_bf16.shape[1]
    width = n // 4
    tm = _tile(t, 1024)
    tn = _tile(width, 1024)
    return pl.pallas_call(
        functools.partial(_proj_a_kernel, q_blocks=width // tn,
                          q_scale=HEAD_DIM ** -0.5 * LOG2_E),
        out_shape=jax.ShapeDtypeStruct((t, n), BF16),
        grid=(t // tm, n // tn),
        in_specs=[
            pl.BlockSpec((tm, d), lambda i, j: (i, 0)),
            pl.BlockSpec((1, d), lambda i, j: (0, 0)),
            pl.BlockSpec((d, tn), lambda i, j: (0, j)),
        ],
        out_specs=pl.BlockSpec((tm, tn), lambda i, j: (i, j)),
        scratch_shapes=[pltpu.VMEM((tm, d), BF16)],
        compiler_params=_params("parallel", "arbitrary"),
        name="proj_a",
    )(x2d, gain.reshape(1, d), w_bf16)


def _sb_attn_kernel(q_ref, k_ref, v_ref, sg_ref, upper_ref, o_ref, vt_scr, acc_scr, carry_scr,
                    *, tq, tk):
    i = pl.program_id(2)
    nsub = tq // tk

    @pl.when(i == 0)
    def _():
        for c in range(vt_scr.shape[0]):
            vt_scr[c] = v_ref[c * tk:(c + 1) * tk, :].T

    q_t = q_ref[...].T
    upper = upper_ref[...]

    def scores(blk):
        start = pl.multiple_of(blk * tk, tk)
        return _dot(k_ref[pl.ds(start, tk), :], q_t)

    def stage(z, strict):
        soft = jnp.log2(1.0 + jnp.exp2(-jnp.abs(z)))
        log_beta = jnp.minimum(z, 0.0) - soft
        log_keep = log_beta - z
        if strict is not None:
            log_keep = jnp.where(strict, log_keep, 0.0)
        later = _dot(upper, log_keep.astype(BF16))
        return log_beta + later, jnp.sum(log_keep, axis=0, keepdims=True)

    def group(blocks, carry, acc):
        raw = [scores(blk) for blk, _ in blocks]
        staged = [stage(z, strict) for z, (_, strict) in zip(raw, blocks)]
        for (blk, strict), (pre, total) in zip(blocks, staged):
            w = jnp.exp2(pre + carry)
            if strict is not None:
                w = jnp.where(strict, w, 0.0)
            acc = acc + _dot(vt_scr[blk], w.astype(BF16))
            carry = carry + total
        acc_scr[...] = acc
        carry_scr[...] = carry

    key = lax.broadcasted_iota(jnp.int32, (tk, tq), 0)
    qry = lax.broadcasted_iota(jnp.int32, (tk, tq), 1)
    diagonal = [(i * nsub + c, (key + c * tk) < qry) for c in reversed(range(nsub))]
    zero_carry = jnp.zeros((1, tq), F32)
    zero_acc = jnp.zeros((HEAD_DIM, tq), F32)

    @pl.when(i == 0)
    def _():
        group(diagonal, zero_carry, zero_acc)

    @pl.when(i > 0)
    def _():
        group(diagonal + [(i * nsub - 1, None)], zero_carry, zero_acc)

    def live(blk):
        return (blk >= 0) & (jnp.max(carry_scr[...]) > EXP2_UNDERFLOW)

    def body(blk):
        group([(blk, None)], carry_scr[...], acc_scr[...])
        return blk - 1

    lax.while_loop(live, body, i * nsub - 2)
    o_ref[...] = (acc_scr[...].T * sg_ref[...].astype(F32)).astype(BF16)


def _sb_attention(qkvg, heads):
    b, s, _ = qkvg.shape
    tq = _tile(s, 512)
    tk = _tile(s, 256)
    upper = (lax.broadcasted_iota(jnp.int32, (tk, tk), 1)
             > lax.broadcasted_iota(jnp.int32, (tk, tk), 0)).astype(BF16)
    return pl.pallas_call(
        functools.partial(_sb_attn_kernel, tq=tq, tk=tk),
        out_shape=jax.ShapeDtypeStruct((b, s, heads * HEAD_DIM), BF16),
        grid=(b, heads, s // tq),
        in_specs=[
            pl.BlockSpec((None, tq, HEAD_DIM), lambda bi, h, i: (bi, i, h)),
            pl.BlockSpec((None, s, HEAD_DIM), lambda bi, h, i: (bi, 0, heads + h)),
            pl.BlockSpec((None, s, HEAD_DIM), lambda bi, h, i: (bi, 0, 2 * heads + h)),
            pl.BlockSpec((None, tq, HEAD_DIM), lambda bi, h, i: (bi, i, 3 * heads + h)),
            pl.BlockSpec((tk, tk), lambda bi, h, i: (0, 0)),
        ],
        out_specs=pl.BlockSpec((None, tq, HEAD_DIM), lambda bi, h, i: (bi, i, h)),
        scratch_shapes=[pltpu.VMEM((s // tk, HEAD_DIM, tk), BF16),
                        pltpu.VMEM((HEAD_DIM, tq), F32), pltpu.VMEM((1, tq), F32)],
        compiler_params=_params("parallel", "parallel", "arbitrary"),
        name="sb_attn",
    )(qkvg, qkvg, qkvg, qkvg, upper)


def _out_kernel(og_ref, x_ref, p_ref, wout_ref, gpost_ref, wpp_ref, wpg_ref, gnext_ref,
                xo_ref, *next_refs):
    y = _dot(og_ref[...], wout_ref[...])
    x1 = x_ref[...] + y * _rms_scale(y) * gpost_ref[...]
    gate = _dot(x1.astype(BF16), wpg_ref[...])
    emb = _dot(p_ref[...].astype(BF16), wpp_ref[...])
    x2 = x1 + emb * _sigmoid(gate)
    xo_ref[...] = x2
    if next_refs:
        normed = x2 * _rms_scale(x2)
        for n, ref in enumerate(next_refs):
            ref[...] = (normed * gnext_ref[n:n + 1, :]).astype(BF16)


def _out_proj(og2d, x2d, p2d, w_out, g_post, w_pp, w_pg, next_gains):
    t, d = x2d.shape
    width = og2d.shape[1]
    ple = p2d.shape[1]
    tm = _tile(t, 512)
    n_next = len(next_gains)
    gnext = jnp.stack(next_gains) if n_next else jnp.zeros((1, d), F32)
    row = lambda i: (i, 0)
    outs = pl.pallas_call(
        _out_kernel,
        out_shape=[jax.ShapeDtypeStruct((t, d), F32)]
        + [jax.ShapeDtypeStruct((t, d), BF16)] * n_next,
        grid=(t // tm,),
        in_specs=[
            pl.BlockSpec((tm, width), row),
            pl.BlockSpec((tm, d), row),
            pl.BlockSpec((tm, ple), row),
            _resident((width, d)),
            _resident((1, d)),
            _resident((ple, d)),
            _resident((d, d)),
            _resident(gnext.shape),
        ],
        out_specs=[pl.BlockSpec((tm, d), row)] * (1 + n_next),
        compiler_params=_params("parallel"),
        name="out_proj",
    )(og2d, x2d, p2d, w_out, g_post.reshape(1, d), w_pp, w_pg, gnext)
    return outs[0], list(outs[1:])


def _kv_kernel(h_ref, wd_ref, gl_ref, wuk_ref, wuvt_ref, cos_ref, sin_ref, k_ref, vt_ref,
               *, rank, heads):
    c = _dot(h_ref[...], wd_ref[...])
    ckv = c[:, :rank]
    k_rope = c[:, rank:rank + ROPE_DIM]
    k_rot = c[:, rank + 128:rank + 128 + ROPE_DIM]
    latent = (ckv * _rms_scale(ckv) * gl_ref[...]).astype(BF16)
    k_nope = _dot(latent, wuk_ref[...]).astype(BF16)
    vt_ref[...] = _dot_nt(wuvt_ref[...], latent).astype(BF16)
    roped = (k_rope * cos_ref[...] + k_rot * sin_ref[...]).astype(BF16)
    for h in range(heads):
        k_ref[h, :, 0:HEAD_DIM] = k_nope[:, h * HEAD_DIM:(h + 1) * HEAD_DIM]
        k_ref[h, :, HEAD_DIM:QK_DIM] = roped


def _kv_side(hkv, wd, g_latent, w_uk, w_uv_t, cos, sin, heads):
    b, s, d = hkv.shape
    rank = w_uk.shape[0]
    tm = _tile(s, 512)
    row = lambda bi, i: (bi, i, 0)
    return pl.pallas_call(
        functools.partial(_kv_kernel, rank=rank, heads=heads),
        out_shape=[jax.ShapeDtypeStruct((b, heads, s, QK_DIM), BF16),
                   jax.ShapeDtypeStruct((b, heads * HEAD_DIM, s), BF16)],
        grid=(b, s // tm),
        in_specs=[
            pl.BlockSpec((None, tm, d), row),
            _resident(wd.shape),
            _resident((1, rank)),
            _resident(w_uk.shape),
            _resident(w_uv_t.shape),
            pl.BlockSpec((tm, ROPE_DIM), lambda bi, i: (i, 0)),
            pl.BlockSpec((tm, ROPE_DIM), lambda bi, i: (i, 0)),
        ],
        out_specs=[pl.BlockSpec((None, heads, tm, QK_DIM), lambda bi, i: (bi, 0, i, 0)),
                   pl.BlockSpec((None, heads * HEAD_DIM, tm), lambda bi, i: (bi, 0, i))],
        compiler_params=_params("parallel", "parallel"),
        name="kv_side",
    )(hkv, wd, g_latent.reshape(1, rank), w_uk, w_uv_t, cos, sin)


def _q_kernel(h_ref, win_ref, gq_ref, wnt_ref, wrt_ref, wrott_ref, cost_ref, sint_ref,
              qt_ref, sg_ref, *, rank, heads, scale):
    c = _dot(h_ref[...], win_ref[...])
    cq = c[:, :rank]
    gate = c[:, rank:]
    sg_ref[...] = (gate * _sigmoid(gate)).astype(BF16)
    latent = (cq * _rms_scale(cq) * gq_ref[...]).astype(BF16)
    nope_t = (_dot_nt(wnt_ref[...], latent) * scale).astype(BF16)
    cos_t = jnp.tile(cost_ref[...], (heads, 1))
    sin_t = jnp.tile(sint_ref[...], (heads, 1))
    roped_t = ((_dot_nt(wrt_ref[...], latent) * cos_t + _dot_nt(wrott_ref[...], latent) * sin_t)
               * scale).astype(BF16)
    for h in range(heads):
        qt_ref[h, 0:HEAD_DIM, :] = nope_t[h * HEAD_DIM:(h + 1) * HEAD_DIM, :]
        qt_ref[h, HEAD_DIM:QK_DIM, :] = roped_t[h * ROPE_DIM:(h + 1) * ROPE_DIM, :]


def _q_side(h, w_in, g_latent, w_n_t, w_r_t, w_rot_t, cos_t, sin_t, heads):
    b, s, d = h.shape
    rank = w_n_t.shape[1]
    width = w_in.shape[1] - rank
    tm = _tile(s, 512)
    row = lambda bi, i: (bi, i, 0)
    return pl.pallas_call(
        functools.partial(_q_kernel, rank=rank, heads=heads, scale=QK_DIM ** -0.5 * LOG2_E),
        out_shape=[jax.ShapeDtypeStruct((b, heads, QK_DIM, s), BF16),
                   jax.ShapeDtypeStruct((b, s, width), BF16)],
        grid=(b, s // tm),
        in_specs=[
            pl.BlockSpec((None, tm, d), row),
            _resident(w_in.shape),
            _resident((1, rank)),
            _resident(w_n_t.shape),
            _resident(w_r_t.shape),
            _resident(w_rot_t.shape),
            pl.BlockSpec((ROPE_DIM, tm), lambda bi, i: (0, i)),
            pl.BlockSpec((ROPE_DIM, tm), lambda bi, i: (0, i)),
        ],
        out_specs=[pl.BlockSpec((None, heads, QK_DIM, tm), lambda bi, i: (bi, 0, 0, i)),
                   pl.BlockSpec((None, tm, width), row)],
        compiler_params=_params("parallel", "parallel"),
        name="q_side",
    )(h, w_in, g_latent.reshape(1, rank), w_n_t, w_r_t, w_rot_t, cos_t, sin_t)


def _mla_attn_kernel(qt_ref, k_ref, vt_ref, sg_ref, o_ref, acc_scr, m_scr, l_scr, *, tq, tk):
    i = pl.program_id(2)
    nsub = tq // tk
    q_t = qt_ref[...]
    acc_scr[...] = jnp.zeros_like(acc_scr)
    m_scr[...] = jnp.full_like(m_scr, NEG_INF)
    l_scr[...] = jnp.zeros_like(l_scr)

    def scores(blk):
        start = pl.multiple_of(blk * tk, tk)
        return _dot(k_ref[pl.ds(start, tk), :], q_t)

    def partial_softmax(blk, s, visible):
        start = pl.multiple_of(blk * tk, tk)
        if visible is not None:
            s = jnp.where(visible, s, NEG_INF)
        m_c = jnp.max(s, axis=0, keepdims=True)
        prob = jnp.exp2(s - m_c)
        l_c = jnp.sum(prob, axis=0, keepdims=True)
        o_c = _dot(vt_ref[:, pl.ds(start, tk)], prob.astype(BF16))
        return m_c, l_c, o_c

    def super_block(sb, masks):
        raw = [scores(sb * nsub + c) for c in range(nsub)]
        parts = [partial_softmax(sb * nsub + c, raw[c], None if masks is None else masks[c])
                 for c in range(nsub)]
        m_prev = m_scr[...]
        m_new = m_prev
        for m_c, _, _ in parts:
            m_new = jnp.maximum(m_new, m_c)
        alpha = jnp.exp2(m_prev - m_new)
        acc = acc_scr[...] * alpha
        l_run = l_scr[...] * alpha
        for m_c, l_c, o_c in parts:
            weight = jnp.exp2(m_c - m_new)
            acc = acc + o_c * weight
            l_run = l_run + l_c * weight
        acc_scr[...] = acc
        l_scr[...] = l_run
        m_scr[...] = m_new

    def body(sb, _):
        super_block(sb, None)
        return 0

    lax.fori_loop(0, i, body, 0)
    kchunk = lax.broadcasted_iota(jnp.int32, (tk, tq), 0) // CHUNK
    qchunk = lax.broadcasted_iota(jnp.int32, (tk, tq), 1) // CHUNK
    super_block(i, [(kchunk + c * (tk // CHUNK)) <= qchunk for c in range(nsub)])
    out = (acc_scr[...] / l_scr[...]).T
    o_ref[...] = (out * sg_ref[...].astype(F32)).astype(BF16)


def _mla_attention(q_t, k_cat, v_t, sg):
    b, heads, _, s = q_t.shape
    tq = _tile(s, 512)
    tk = _tile(s, 256)
    return pl.pallas_call(
        functools.partial(_mla_attn_kernel, tq=tq, tk=tk),
        out_shape=jax.ShapeDtypeStruct((b, s, heads * HEAD_DIM), BF16),
        grid=(b, heads, s // tq),
        in_specs=[
            pl.BlockSpec((None, None, QK_DIM, tq), lambda bi, h, i: (bi, h, 0, i)),
            pl.BlockSpec((None, None, s, QK_DIM), lambda bi, h, i: (bi, h, 0, 0)),
            pl.BlockSpec((None, HEAD_DIM, s), lambda bi, h, i: (bi, h, 0)),
            pl.BlockSpec((None, tq, HEAD_DIM), lambda bi, h, i: (bi, i, h)),
        ],
        out_specs=pl.BlockSpec((None, tq, HEAD_DIM), lambda bi, h, i: (bi, i, h)),
        scratch_shapes=[pltpu.VMEM((HEAD_DIM, tq), F32), pltpu.VMEM((1, tq), F32),
                        pltpu.VMEM((1, tq), F32)],
        compiler_params=_params("parallel", "parallel", "arbitrary"),
        name="mla_attn",
    )(q_t, k_cat, v_t, sg)


def _rotate_half_columns(w):
    half = ROPE_DIM // 2
    return jnp.concatenate([-w[..., half:], w[..., :half]], axis=-1)


def _rope_tables(s):
    inv = 1.0 / (ROPE_THETA ** (jnp.arange(0, ROPE_DIM, 2, dtype=F32) / ROPE_DIM))
    ang = jnp.arange(s, dtype=F32)[:, None] * inv[None, :]
    cos, sin = jnp.cos(ang), jnp.sin(ang)
    return jnp.tile(cos, (1, 2)), jnp.tile(sin, (1, 2))


def kernel(x, p, norm_pre, norm_post, w_in_a, w_out_a, w_in_b, q_latent_norm, w_uq, w_out_b,
           kv_norm, w_dkv, kv_latent_norm, w_uk, w_uv, w_ple_proj, w_ple_gate):
    b, s, d = x.shape
    t = b * s
    n_a, n_b = w_in_a.shape[0], w_in_b.shape[0]
    heads = d // HEAD_DIM
    q_rank = w_uq.shape[1]
    kv_rank = w_uk.shape[0]
    cos, sin = _rope_tables(s)

    x2d = x.reshape(t, d)
    p2d = p.reshape(p.shape[0], t, p.shape[-1])

    def finish_layer(layer, og, x2d, w_out):
        if layer + 1 < n_a:
            gains = [norm_pre[layer + 1]]
        elif layer + 1 == n_a and n_b:
            gains = [norm_pre[layer + 1], kv_norm]
        elif layer + 1 < n_a + n_b:
            gains = [norm_pre[layer + 1]]
        else:
            gains = []
        return _out_proj(og.reshape(t, -1), x2d, p2d[layer], w_out.astype(BF16),
                         norm_post[layer], w_ple_proj[layer].astype(BF16),
                         w_ple_gate[layer].astype(BF16), gains)

    h_next = None
    for i in range(n_a):
        qkvg = _proj_a(x2d, norm_pre[i], w_in_a[i].astype(BF16))
        og = _sb_attention(qkvg.reshape(b, s, -1), heads)
        x2d, h_next = finish_layer(i, og, x2d, w_out_a[i])

    if n_b == 0:
        return x2d.reshape(b, s, d)
    if n_a == 0:
        raise NotImplementedError("a trunk without mixer-A layers is not supported")

    h_b, h_kv = h_next
    wd = jnp.concatenate([
        w_dkv[:, :kv_rank],
        jnp.pad(w_dkv[:, kv_rank:], ((0, 0), (0, 128 - ROPE_DIM))),
        jnp.pad(_rotate_half_columns(w_dkv[:, kv_rank:]), ((0, 0), (0, 128 - ROPE_DIM))),
    ], axis=1).astype(BF16)
    k_cat, v_t = _kv_side(h_kv.reshape(b, s, d), wd, kv_latent_norm, w_uk.astype(BF16),
                          w_uv.T.astype(BF16), cos, sin, heads)

    for j in range(n_b):
        layer = n_a + j
        wq = w_uq[j].reshape(q_rank, heads, QK_DIM)
        w_n_t = wq[:, :, :HEAD_DIM].reshape(q_rank, heads * HEAD_DIM).T.astype(BF16)
        w_r_t = wq[:, :, HEAD_DIM:].reshape(q_rank, heads * ROPE_DIM).T.astype(BF16)
        w_rot_t = _rotate_half_columns(wq[:, :, HEAD_DIM:]).reshape(
            q_rank, heads * ROPE_DIM).T.astype(BF16)
        q_t, sg = _q_side(h_b.reshape(b, s, d), w_in_b[j].astype(BF16), q_latent_norm[j],
                          w_n_t, w_r_t, w_rot_t, cos.T, sin.T, heads)
        og = _mla_attention(q_t, k_cat, v_t, sg)
        x2d, h_next = finish_layer(layer, og, x2d, w_out_b[j])
        if h_next:
            h_b = h_next[0]
    return x2d.reshape(b, s, d)
```

```python
import functools
import math

import jax
import jax.numpy as jnp
from jax import lax
from jax.experimental import pallas as pl
from jax.experimental.pallas import tpu as pltpu

RMS_EPS = 1e-6
NEG_INF = -1e30
HEAD_DIM = 128
ROPE_DIM = 64
QK_DIM = HEAD_DIM + ROPE_DIM
CHUNK = 64
ROPE_THETA = 10000.0
LOG2_E = math.log2(math.e)
EXP2_UNDERFLOW = -160.0

V7X_VMEM_LIMIT_BYTES = 56 * 1024 * 1024
BF16 = jnp.bfloat16
F32 = jnp.float32


def _tile(n, pref):
    t = min(n, pref)
    while n % t:
        t //= 2
    return t


def _params(*sem, flags=None):
    return pltpu.CompilerParams(dimension_semantics=sem,
                                vmem_limit_bytes=V7X_VMEM_LIMIT_BYTES, flags=flags)


def _resident(shape):
    nd = len(shape)
    return pl.BlockSpec(shape, lambda *_: (0,) * nd, pipeline_mode=pl.Buffered(1))


def _rms_scale(v):
    return lax.rsqrt(jnp.mean(v * v, axis=-1, keepdims=True) + RMS_EPS)


def _sigmoid(v):
    return 1.0 / (1.0 + jnp.exp(-v))


def _dot(a, b):
    return jnp.dot(a, b, preferred_element_type=F32)


def _dot_nt(a, b):
    return lax.dot_general(a, b, (((1,), (1,)), ((), ())), preferred_element_type=F32)


def _proj_a_kernel(x_ref, g_ref, w_ref, o_ref, h_scr, *, q_blocks, q_scale):
    j = pl.program_id(1)

    @pl.when(j == 0)
    def _():
        x = x_ref[...]
        h_scr[...] = (x * _rms_scale(x) * g_ref[...]).astype(BF16)

    acc = _dot(h_scr[...], w_ref[...])

    @pl.when(j < q_blocks)
    def _():
        o_ref[...] = (acc * q_scale).astype(BF16)

    @pl.when((j >= q_blocks) & (j < 3 * q_blocks))
    def _():
        o_ref[...] = acc.astype(BF16)

    @pl.when(j >= 3 * q_blocks)
    def _():
        o_ref[...] = (acc * _sigmoid(acc)).astype(BF16)


def _proj_a(x2d, gain, w_bf16):
    t, d = x2d.shape
    n = w_bf16.shape[1]
    width = n // 4
    tm = _tile(t, 1024)
    tn = _tile(width, 1024)
    return pl.pallas_call(
        functools.partial(_proj_a_kernel, q_blocks=width // tn,
                          q_scale=HEAD_DIM ** -0.5 * LOG2_E),
        out_shape=jax.ShapeDtypeStruct((t, n), BF16),
        grid=(t // tm, n // tn),
        in_specs=[
            pl.BlockSpec((tm, d), lambda i, j: (i, 0)),
            pl.BlockSpec((1, d), lambda i, j: (0, 0)),
            pl.BlockSpec((d, tn), lambda i, j: (0, j)),
        ],
        out_specs=pl.BlockSpec((tm, tn), lambda i, j: (i, j)),
        scratch_shapes=[pltpu.VMEM((tm, d), BF16)],
        compiler_params=_params("parallel", "arbitrary"),
        name="proj_a",
    )(x2d, gain.reshape(1, d), w_bf16)


def _sb_attn_kernel(q_ref, k_ref, v_ref, sg_ref, upper_ref, o_ref, vt_scr, acc_scr, carry_scr,
                    *, tq, tk):
    i = pl.program_id(2)
    nsub = tq // tk

    @pl.when(i == 0)
    def _():
        for c in range(vt_scr.shape[0]):
            vt_scr[c] = v_ref[c * tk:(c + 1) * tk, :].T

    q_t = q_ref[...].T
    upper = upper_ref[...]

    def scores(blk):
        start = pl.multiple_of(blk * tk, tk)
        return _dot(k_ref[pl.ds(start, tk), :], q_t)

    def stage(z, strict):
        soft = jnp.log2(1.0 + jnp.exp2(-jnp.abs(z)))
        log_beta = jnp.minimum(z, 0.0) - soft
        log_keep = log_beta - z
        if strict is not None:
            log_keep = jnp.where(strict, log_keep, 0.0)
        later = _dot(upper, log_keep.astype(BF16))
        return log_beta + later, jnp.sum(log_keep, axis=0, keepdims=True)

    def group(blocks, carry, acc):
        raw = [scores(blk) for blk, _ in blocks]
        staged = [stage(z, strict) for z, (_, strict) in zip(raw, blocks)]
        for (blk, strict), (pre, total) in zip(blocks, staged):
            w = jnp.exp2(pre + carry)
            if strict is not None:
                w = jnp.where(strict, w, 0.0)
            acc = acc + _dot(vt_scr[blk], w.astype(BF16))
            carry = carry + total
        acc_scr[...] = acc
        carry_scr[...] = carry

    key = lax.broadcasted_iota(jnp.int32, (tk, tq), 0)
    qry = lax.broadcasted_iota(jnp.int32, (tk, tq), 1)
    diagonal = [(i * nsub + c, (key + c * tk) < qry) for c in reversed(range(nsub))]
    zero_carry = jnp.zeros((1, tq), F32)
    zero_acc = jnp.zeros((HEAD_DIM, tq), F32)

    @pl.when(i == 0)
    def _():
        group(diagonal, zero_carry, zero_acc)

    @pl.when(i > 0)
    def _():
        group(diagonal + [(i * nsub - 1, None)], zero_carry, zero_acc)

    def live(blk):
        return (blk >= 0) & (jnp.max(carry_scr[...]) > EXP2_UNDERFLOW)

    def body(blk):
        group([(blk, None)], carry_scr[...], acc_scr[...])
        return blk - 1

    lax.while_loop(live, body, i * nsub - 2)
    o_ref[...] = (acc_scr[...].T * sg_ref[...].astype(F32)).astype(BF16)


def _sb_attention(qkvg, heads):
    b, s, _ = qkvg.shape
    tq = _tile(s, 512)
    tk = _tile(s, 256)
    upper = (lax.broadcasted_iota(jnp.int32, (tk, tk), 1)
             > lax.broadcasted_iota(jnp.int32, (tk, tk), 0)).astype(BF16)
    return pl.pallas_call(
        functools.partial(_sb_attn_kernel, tq=tq, tk=tk),
        out_shape=jax.ShapeDtypeStruct((b, s, heads * HEAD_DIM), BF16),
        grid=(b, heads, s // tq),
        in_specs=[
            pl.BlockSpec((None, tq, HEAD_DIM), lambda bi, h, i: (bi, i, h)),
            pl.BlockSpec((None, s, HEAD_DIM), lambda bi, h, i: (bi, 0, heads + h)),
            pl.BlockSpec((None, s, HEAD_DIM), lambda bi, h, i: (bi, 0, 2 * heads + h)),
            pl.BlockSpec((None, tq, HEAD_DIM), lambda bi, h, i: (bi, i, 3 * heads + h)),
            pl.BlockSpec((tk, tk), lambda bi, h, i: (0, 0)),
        ],
        out_specs=pl.BlockSpec((None, tq, HEAD_DIM), lambda bi, h, i: (bi, i, h)),
        scratch_shapes=[pltpu.VMEM((s // tk, HEAD_DIM, tk), BF16),
                        pltpu.VMEM((HEAD_DIM, tq), F32), pltpu.VMEM((1, tq), F32)],
        compiler_params=_params("parallel", "parallel", "arbitrary"),
        name="sb_attn",
    )(qkvg, qkvg, qkvg, qkvg, upper)


def _out_kernel(og_ref, x_ref, p_ref, wout_ref, gpost_ref, wpp_ref, wpg_ref, gnext_ref,
                xo_ref, *next_refs):
    y = _dot(og_ref[...], wout_ref[...])
    x1 = x_ref[...] + y * _rms_scale(y) * gpost_ref[...]
    gate = _dot(x1.astype(BF16), wpg_ref[...])
    emb = _dot(p_ref[...].astype(BF16), wpp_ref[...])
    x2 = x1 + emb * _sigmoid(gate)
    xo_ref[...] = x2
    if next_refs:
        normed = x2 * _rms_scale(x2)
        for n, ref in enumerate(next_refs):
            ref[...] = (normed * gnext_ref[n:n + 1, :]).astype(BF16)


def _out_proj(og2d, x2d, p2d, w_out, g_post, w_pp, w_pg, next_gains):
    t, d = x2d.shape
    width = og2d.shape[1]
    ple = p2d.shape[1]
    tm = _tile(t, 512)
    n_next = len(next_gains)
    gnext = jnp.stack(next_gains) if n_next else jnp.zeros((1, d), F32)
    row = lambda i: (i, 0)
    outs = pl.pallas_call(
        _out_kernel,
        out_shape=[jax.ShapeDtypeStruct((t, d), F32)]
        + [jax.ShapeDtypeStruct((t, d), BF16)] * n_next,
        grid=(t // tm,),
        in_specs=[
            pl.BlockSpec((tm, width), row),
            pl.BlockSpec((tm, d), row),
            pl.BlockSpec((tm, ple), row),
            _resident((width, d)),
            _resident((1, d)),
            _resident((ple, d)),
            _resident((d, d)),
            _resident(gnext.shape),
        ],
        out_specs=[pl.BlockSpec((tm, d), row)] * (1 + n_next),
        compiler_params=_params("parallel"),
        name="out_proj",
    )(og2d, x2d, p2d, w_out, g_post.reshape(1, d), w_pp, w_pg, gnext)
    return outs[0], list(outs[1:])


def _kv_kernel(h_ref, wd_ref, gl_ref, wuk_ref, wuvt_ref, cos_ref, sin_ref, k_ref, vt_ref,
               *, rank, heads):
    c = _dot(h_ref[...], wd_ref[...])
    ckv = c[:, :rank]
    k_rope = c[:, rank:rank + ROPE_DIM]
    k_rot = c[:, rank + 128:rank + 128 + ROPE_DIM]
    latent = (ckv * _rms_scale(ckv) * gl_ref[...]).astype(BF16)
    k_nope = _dot(latent, wuk_ref[...]).astype(BF16)
    vt_ref[...] = _dot_nt(wuvt_ref[...], latent).astype(BF16)
    roped = (k_rope * cos_ref[...] + k_rot * sin_ref[...]).astype(BF16)
    for h in range(heads):
        k_ref[h, :, 0:HEAD_DIM] = k_nope[:, h * HEAD_DIM:(h + 1) * HEAD_DIM]
        k_ref[h, :, HEAD_DIM:QK_DIM] = roped


def _kv_side(hkv, wd, g_latent, w_uk, w_uv_t, cos, sin, heads):
    b, s, d = hkv.shape
    rank = w_uk.shape[0]
    tm = _tile(s, 512)
    row = lambda bi, i: (bi, i, 0)
    return pl.pallas_call(
        functools.partial(_kv_kernel, rank=rank, heads=heads),
        out_shape=[jax.ShapeDtypeStruct((b, heads, s, QK_DIM), BF16),
                   jax.ShapeDtypeStruct((b, heads * HEAD_DIM, s), BF16)],
        grid=(b, s // tm),
        in_specs=[
            pl.BlockSpec((None, tm, d), row),
            _resident(wd.shape),
            _resident((1, rank)),
            _resident(w_uk.shape),
            _resident(w_uv_t.shape),
            pl.BlockSpec((tm, ROPE_DIM), lambda bi, i: (i, 0)),
            pl.BlockSpec((tm, ROPE_DIM), lambda bi, i: (i, 0)),
        ],
        out_specs=[pl.BlockSpec((None, heads, tm, QK_DIM), lambda bi, i: (bi, 0, i, 0)),
                   pl.BlockSpec((None, heads * HEAD_DIM, tm), lambda bi, i: (bi, 0, i))],
        compiler_params=_params("parallel", "parallel"),
        name="kv_side",
    )(hkv, wd, g_latent.reshape(1, rank), w_uk, w_uv_t, cos, sin)


def _q_kernel(h_ref, win_ref, gq_ref, wnt_ref, wrt_ref, wrott_ref, cost_ref, sint_ref,
              qt_ref, sg_ref, *, rank, heads, scale):
    c = _dot(h_ref[...], win_ref[...])
    cq = c[:, :rank]
    gate = c[:, rank:]
    sg_ref[...] = (gate * _sigmoid(gate)).astype(BF16)
    latent = (cq * _rms_scale(cq) * gq_ref[...]).astype(BF16)
    nope_t = (_dot_nt(wnt_ref[...], latent) * scale).astype(BF16)
    cos_t = jnp.tile(cost_ref[...], (heads, 1))
    sin_t = jnp.tile(sint_ref[...], (heads, 1))
    roped_t = ((_dot_nt(wrt_ref[...], latent) * cos_t + _dot_nt(wrott_ref[...], latent) * sin_t)
               * scale).astype(BF16)
    for h in range(heads):
        qt_ref[h, 0:HEAD_DIM, :] = nope_t[h * HEAD_DIM:(h + 1) * HEAD_DIM, :]
        qt_ref[h, HEAD_DIM:QK_DIM, :] = roped_t[h * ROPE_DIM:(h + 1) * ROPE_DIM, :]


def _q_side(h, w_in, g_latent, w_n_t, w_r_t, w_rot_t, cos_t, sin_t, heads):
    b, s, d = h.shape
    rank = w_n_t.shape[1]
    width = w_in.shape[1] - rank
    tm = _tile(s, 512)
    row = lambda bi, i: (bi, i, 0)
    return pl.pallas_call(
        functools.partial(_q_kernel, rank=rank, heads=heads, scale=QK_DIM ** -0.5 * LOG2_E),
        out_shape=[jax.ShapeDtypeStruct((b, heads, QK_DIM, s), BF16),
                   jax.ShapeDtypeStruct((b, s, width), BF16)],
        grid=(b, s // tm),
        in_specs=[
            pl.BlockSpec((None, tm, d), row),
            _resident(w_in.shape),
            _resident((1, rank)),
            _resident(w_n_t.shape),
            _resident(w_r_t.shape),
            _resident(w_rot_t.shape),
            pl.BlockSpec((ROPE_DIM, tm), lambda bi, i: (0, i)),
            pl.BlockSpec((ROPE_DIM, tm), lambda bi, i: (0, i)),
        ],
        out_specs=[pl.BlockSpec((None, heads, QK_DIM, tm), lambda bi, i: (bi, 0, 0, i)),
                   pl.BlockSpec((None, tm, width), row)],
        compiler_params=_params("parallel", "parallel"),
        name="q_side",
    )(h, w_in, g_latent.reshape(1, rank), w_n_t, w_r_t, w_rot_t, cos_t, sin_t)


def _mla_attn_kernel(qt_ref, k_ref, vt_ref, sg_ref, o_ref, s_scr, p_scr, acc_scr, *, tq, tk):
    i = pl.program_id(2)
    nblk = (i + 1) * (tq // tk)
    q_t = qt_ref[...]
    acc_scr[...] = jnp.zeros_like(acc_scr)

    def key_start(n):
        return pl.multiple_of(n * tk, tk)

    def step(n, parity, stats, run, *, merge, softmax, score, mask=None):
        if score:
            s_scr[parity] = _dot(k_ref[pl.ds(key_start(n), tk), :], q_t)
        if merge:
            out = _dot(vt_ref[:, pl.ds(key_start(n - 2), tk)], p_scr[parity])
        m_c, l_c = stats
        if softmax:
            s = s_scr[1 - parity]
            if mask is not None:
                s = jnp.where(mask, s, NEG_INF)
            m_next = jnp.max(s, axis=0, keepdims=True)
            prob = jnp.exp2(s - m_next)
            p_scr[1 - parity] = prob.astype(BF16)
            stats = (m_next, jnp.sum(prob, axis=0, keepdims=True))
        if merge:
            m_run, l_run = run
            m_new = jnp.maximum(m_run, m_c)
            alpha = jnp.exp2(m_run - m_new)
            weight = jnp.exp2(m_c - m_new)
            acc_scr[...] = acc_scr[...] * alpha + out * weight
            run = (m_new, l_run * alpha + l_c * weight)
        return stats, run

    def finish(run):
        out = (acc_scr[...] / run[1]).T
        o_ref[...] = (out * sg_ref[...].astype(F32)).astype(BF16)

    kchunk = lax.broadcasted_iota(jnp.int32, (tk, tq), 0) // CHUNK
    qchunk = lax.broadcasted_iota(jnp.int32, (tk, tq), 1) // CHUNK
    first_diag = kchunk <= qchunk
    second_diag = (kchunk + tk // CHUNK) <= qchunk
    row = jnp.zeros((1, tq), F32)
    state = ((row, row), (jnp.full((1, tq), NEG_INF, F32), row))
    everything = dict(merge=True, softmax=True, score=True)

    @pl.when(i == 0)
    def _():
        st = step(0, 0, *state, merge=False, softmax=False, score=True)
        st = step(1, 1, *st, merge=False, softmax=True, score=True, mask=first_diag)
        st = step(2, 0, *st, merge=True, softmax=True, score=False, mask=second_diag)
        st = step(3, 1, *st, merge=True, softmax=False, score=False)
        finish(st[1])

    @pl.when(i > 0)
    def _():
        st = step(0, 0, *state, merge=False, softmax=False, score=True)
        st = step(1, 1, *st, merge=False, softmax=True, score=True)
        st = step(2, 0, *st, **everything)

        def pair(t, st):
            st = step(3 + 2 * t, 1, *st, **everything)
            return step(4 + 2 * t, 0, *st, **everything)

        st = lax.fori_loop(0, (nblk - 4) // 2, pair, st)
        st = step(nblk - 1, 1, *st, mask=first_diag, **everything)
        st = step(nblk, 0, *st, merge=True, softmax=True, score=False, mask=second_diag)
        st = step(nblk + 1, 1, *st, merge=True, softmax=False, score=False)
        finish(st[1])


def _mla_attention(q_t, k_cat, v_t, sg):
    b, heads, _, s = q_t.shape
    tq = _tile(s, 512)
    tk = _tile(s, 256)
    assert tq == 2 * tk, "the pipeline below peels exactly two diagonal key blocks"
    return pl.pallas_call(
        functools.partial(_mla_attn_kernel, tq=tq, tk=tk),
        out_shape=jax.ShapeDtypeStruct((b, s, heads * HEAD_DIM), BF16),
        grid=(b, heads, s // tq),
        in_specs=[
            pl.BlockSpec((None, None, QK_DIM, tq), lambda bi, h, i: (bi, h, 0, i)),
            pl.BlockSpec((None, None, s, QK_DIM), lambda bi, h, i: (bi, h, 0, 0)),
            pl.BlockSpec((None, HEAD_DIM, s), lambda bi, h, i: (bi, h, 0)),
            pl.BlockSpec((None, tq, HEAD_DIM), lambda bi, h, i: (bi, i, h)),
        ],
        out_specs=pl.BlockSpec((None, tq, HEAD_DIM), lambda bi, h, i: (bi, i, h)),
        scratch_shapes=[pltpu.VMEM((2, tk, tq), F32), pltpu.VMEM((2, tk, tq), BF16),
                        pltpu.VMEM((HEAD_DIM, tq), F32)],
        compiler_params=_params("parallel", "parallel", "arbitrary"),
        name="mla_attn",
    )(q_t, k_cat, v_t, sg)


def _rotate_half_columns(w):
    half = ROPE_DIM // 2
    return jnp.concatenate([-w[..., half:], w[..., :half]], axis=-1)


def _rope_tables(s):
    inv = 1.0 / (ROPE_THETA ** (jnp.arange(0, ROPE_DIM, 2, dtype=F32) / ROPE_DIM))
    ang = jnp.arange(s, dtype=F32)[:, None] * inv[None, :]
    cos, sin = jnp.cos(ang), jnp.sin(ang)
    return jnp.tile(cos, (1, 2)), jnp.tile(sin, (1, 2))


def kernel(x, p, norm_pre, norm_post, w_in_a, w_out_a, w_in_b, q_latent_norm, w_uq, w_out_b,
           kv_norm, w_dkv, kv_latent_norm, w_uk, w_uv, w_ple_proj, w_ple_gate):
    b, s, d = x.shape
    t = b * s
    n_a, n_b = w_in_a.shape[0], w_in_b.shape[0]
    heads = d // HEAD_DIM
    q_rank = w_uq.shape[1]
    kv_rank = w_uk.shape[0]
    cos, sin = _rope_tables(s)

    x2d = x.reshape(t, d)
    p2d = p.reshape(p.shape[0], t, p.shape[-1])

    def finish_layer(layer, og, x2d, w_out):
        if layer + 1 < n_a:
            gains = [norm_pre[layer + 1]]
        elif layer + 1 == n_a and n_b:
            gains = [norm_pre[layer + 1], kv_norm]
        elif layer + 1 < n_a + n_b:
            gains = [norm_pre[layer + 1]]
        else:
            gains = []
        return _out_proj(og.reshape(t, -1), x2d, p2d[layer], w_out.astype(BF16),
                         norm_post[layer], w_ple_proj[layer].astype(BF16),
                         w_ple_gate[layer].astype(BF16), gains)

    h_next = None
    for i in range(n_a):
        qkvg = _proj_a(x2d, norm_pre[i], w_in_a[i].astype(BF16))
        og = _sb_attention(qkvg.reshape(b, s, -1), heads)
        x2d, h_next = finish_layer(i, og, x2d, w_out_a[i])

    if n_b == 0:
        return x2d.reshape(b, s, d)
    if n_a == 0:
        raise NotImplementedError("a trunk without mixer-A layers is not supported")

    h_b, h_kv = h_next
    wd = jnp.concatenate([
        w_dkv[:, :kv_rank],
        jnp.pad(w_dkv[:, kv_rank:], ((0, 0), (0, 128 - ROPE_DIM))),
        jnp.pad(_rotate_half_columns(w_dkv[:, kv_rank:]), ((0, 0), (0, 128 - ROPE_DIM))),
    ], axis=1).astype(BF16)
    k_cat, v_t = _kv_side(h_kv.reshape(b, s, d), wd, kv_latent_norm, w_uk.astype(BF16),
                          w_uv.T.astype(BF16), cos, sin, heads)

    for j in range(n_b):
        layer = n_a + j
        wq = w_uq[j].reshape(q_rank, heads, QK_DIM)
        w_n_t = wq[:, :, :HEAD_DIM].reshape(q_rank, heads * HEAD_DIM).T.astype(BF16)
        w_r_t = wq[:, :, HEAD_DIM:].reshape(q_rank, heads * ROPE_DIM).T.astype(BF16)
        w_rot_t = _rotate_half_columns(wq[:, :, HEAD_DIM:]).reshape(
            q_rank, heads * ROPE_DIM).T.astype(BF16)
        q_t, sg = _q_side(h_b.reshape(b, s, d), w_in_b[j].astype(BF16), q_latent_norm[j],
                          w_n_t, w_r_t, w_rot_t, cos.T, sin.T, heads)
        og = _mla_attention(q_t, k_cat, v_t, sg)
        x2d, h_next = finish_layer(layer, og, x2d, w_out_b[j])
        if h_next:
            h_b = h_next[0]
    return x2d.reshape(b, s, d)
```

```python
import functools
import math

import jax
import jax.numpy as jnp
from jax import lax
from jax.experimental import pallas as pl
from jax.experimental.pallas import tpu as pltpu

RMS_EPS = 1e-6
NEG_INF = -1e30
HEAD_DIM = 128
ROPE_DIM = 64
QK_DIM = HEAD_DIM + ROPE_DIM
CHUNK = 64
HEADS_PER_STEP = 2
ROPE_THETA = 10000.0
LOG2_E = math.log2(math.e)
EXP2_UNDERFLOW = -160.0

V7X_VMEM_LIMIT_BYTES = 56 * 1024 * 1024
BF16 = jnp.bfloat16
F32 = jnp.float32


def _tile(n, pref):
    t = min(n, pref)
    while n % t:
        t //= 2
    return t


def _params(*sem):
    return pltpu.CompilerParams(dimension_semantics=sem,
                                vmem_limit_bytes=V7X_VMEM_LIMIT_BYTES)


def _resident(shape):
    nd = len(shape)
    return pl.BlockSpec(shape, lambda *_: (0,) * nd, pipeline_mode=pl.Buffered(1))


def _rms_scale(v):
    return lax.rsqrt(jnp.mean(v * v, axis=-1, keepdims=True) + RMS_EPS)


def _sigmoid(v):
    return 1.0 / (1.0 + jnp.exp(-v))


def _dot(a, b):
    return jnp.dot(a, b, preferred_element_type=F32)


def _dot_nt(a, b):
    return lax.dot_general(a, b, (((1,), (1,)), ((), ())), preferred_element_type=F32)


def _proj_a_kernel(x_ref, g_ref, w_ref, o_ref, h_scr, *, q_blocks, q_scale):
    j = pl.program_id(1)

    @pl.when(j == 0)
    def _():
        x = x_ref[...]
        h_scr[...] = (x * _rms_scale(x) * g_ref[...]).astype(BF16)

    acc = _dot(h_scr[...], w_ref[...].astype(BF16))

    @pl.when(j < q_blocks)
    def _():
        o_ref[...] = (acc * q_scale).astype(BF16)

    @pl.when((j >= q_blocks) & (j < 3 * q_blocks))
    def _():
        o_ref[...] = acc.astype(BF16)

    @pl.when(j >= 3 * q_blocks)
    def _():
        o_ref[...] = (acc * _sigmoid(acc)).astype(BF16)


def _proj_a(x2d, gain, w_in):
    t, d = x2d.shape
    n = w_in.shape[1]
    width = n // 4
    tm = _tile(t, 1024)
    tn = _tile(width, 1024)
    return pl.pallas_call(
        functools.partial(_proj_a_kernel, q_blocks=width // tn,
                          q_scale=HEAD_DIM ** -0.5 * LOG2_E),
        out_shape=jax.ShapeDtypeStruct((t, n), BF16),
        grid=(t // tm, n // tn),
        in_specs=[
            pl.BlockSpec((tm, d), lambda i, j: (i, 0)),
            pl.BlockSpec((1, d), lambda i, j: (0, 0)),
            pl.BlockSpec((d, tn), lambda i, j: (0, j)),
        ],
        out_specs=pl.BlockSpec((tm, tn), lambda i, j: (i, j)),
        scratch_shapes=[pltpu.VMEM((tm, d), BF16)],
        compiler_params=_params("parallel", "arbitrary"),
        name="proj_a",
    )(x2d, gain.reshape(1, d), w_in)


def _sb_attn_kernel(q_ref, k_ref, v_ref, sg_ref, upper_ref, o_ref, vt_scr, acc_scr, carry_scr,
                    *, tq, tk):
    for hh in range(HEADS_PER_STEP):
        cols = pl.ds(hh * HEAD_DIM, HEAD_DIM)
        _sb_head(q_ref.at[:, cols], k_ref.at[:, cols], v_ref.at[:, cols], sg_ref.at[:, cols],
                 upper_ref, o_ref.at[:, cols], vt_scr.at[hh], acc_scr, carry_scr, tq=tq, tk=tk)


def _sb_head(q_ref, k_ref, v_ref, sg_ref, upper_ref, o_ref, vt_scr, acc_scr, carry_scr, *, tq, tk):
    i = pl.program_id(2)
    nsub = tq // tk

    @pl.when(i == 0)
    def _():
        for c in range(vt_scr.shape[0]):
            vt_scr[c] = v_ref[c * tk:(c + 1) * tk, :].T

    q_t = q_ref[...].T
    upper = upper_ref[...]

    def scores(blk, lane0):
        start = pl.multiple_of(blk * tk, tk)
        return _dot(k_ref[pl.ds(start, tk), :], q_t[:, lane0:])

    def stage(z, strict):
        soft = jnp.log2(1.0 + jnp.exp2(-jnp.abs(z)))
        log_beta = jnp.minimum(z, 0.0) - soft
        log_keep = log_beta - z
        if strict is not None:
            log_keep = jnp.where(strict, log_keep, 0.0)
        later = _dot(upper, log_keep.astype(BF16))
        return log_beta + later, jnp.sum(log_keep, axis=0, keepdims=True)

    def group(blocks, carry, acc):
        raw = [scores(blk, lane0) for blk, _, lane0 in blocks]
        staged = [stage(z, strict) for z, (_, strict, _) in zip(raw, blocks)]
        for (blk, strict, lane0), (pre, total) in zip(blocks, staged):
            w = jnp.exp2(pre + carry[:, lane0:])
            if strict is not None:
                w = jnp.where(strict, w, 0.0)
            out = _dot(vt_scr[blk], w.astype(BF16))
            if lane0:
                acc = jnp.concatenate([acc[:, :lane0], acc[:, lane0:] + out], axis=1)
                carry = jnp.concatenate([carry[:, :lane0], carry[:, lane0:] + total], axis=1)
            else:
                acc = acc + out
                carry = carry + total
        acc_scr[...] = acc
        carry_scr[...] = carry

    diagonal = []
    for c in reversed(range(nsub)):
        key = lax.broadcasted_iota(jnp.int32, (tk, tq - c * tk), 0)
        qry = lax.broadcasted_iota(jnp.int32, (tk, tq - c * tk), 1)
        diagonal.append((i * nsub + c, key < qry, c * tk))
    zero_carry = jnp.zeros((1, tq), F32)
    zero_acc = jnp.zeros((HEAD_DIM, tq), F32)

    @pl.when(i == 0)
    def _():
        group(diagonal, zero_carry, zero_acc)

    @pl.when(i > 0)
    def _():
        group(diagonal + [(i * nsub - 1, None, 0)], zero_carry, zero_acc)

    def live(blk):
        return (blk >= 0) & (jnp.max(carry_scr[...]) > EXP2_UNDERFLOW)

    def body(blk):
        group([(blk, None, 0)], carry_scr[...], acc_scr[...])
        return blk - 1

    lax.while_loop(live, body, i * nsub - 2)
    o_ref[...] = (acc_scr[...].T * sg_ref[...].astype(F32)).astype(BF16)


def _sb_attention(qkvg, heads):
    b, s, _ = qkvg.shape
    tq = _tile(s, 512)
    tk = _tile(s, 256)
    groups, width = heads // HEADS_PER_STEP, HEADS_PER_STEP * HEAD_DIM
    upper = (lax.broadcasted_iota(jnp.int32, (tk, tk), 1)
             > lax.broadcasted_iota(jnp.int32, (tk, tk), 0)).astype(BF16)
    return pl.pallas_call(
        functools.partial(_sb_attn_kernel, tq=tq, tk=tk),
        out_shape=jax.ShapeDtypeStruct((b, s, heads * HEAD_DIM), BF16),
        grid=(b, groups, s // tq),
        in_specs=[
            pl.BlockSpec((None, tq, width), lambda bi, h, i: (bi, i, h)),
            pl.BlockSpec((None, s, width), lambda bi, h, i: (bi, 0, groups + h)),
            pl.BlockSpec((None, s, width), lambda bi, h, i: (bi, 0, 2 * groups + h)),
            pl.BlockSpec((None, tq, width), lambda bi, h, i: (bi, i, 3 * groups + h)),
            pl.BlockSpec((tk, tk), lambda bi, h, i: (0, 0)),
        ],
        out_specs=pl.BlockSpec((None, tq, width), lambda bi, h, i: (bi, i, h)),
        scratch_shapes=[pltpu.VMEM((HEADS_PER_STEP, s // tk, HEAD_DIM, tk), BF16),
                        pltpu.VMEM((HEAD_DIM, tq), F32), pltpu.VMEM((1, tq), F32)],
        compiler_params=_params("parallel", "parallel", "arbitrary"),
        name="sb_attn",
    )(qkvg, qkvg, qkvg, qkvg, upper)


def _out_kernel(og_ref, x_ref, p_ref, wout_ref, gpost_ref, wpp_ref, wpg_ref, gnext_ref,
                xo_ref, *next_refs):
    y = _dot(og_ref[...], wout_ref[...])
    x1 = x_ref[...] + y * _rms_scale(y) * gpost_ref[...]
    gate = _dot(x1.astype(BF16), wpg_ref[...])
    emb = _dot(p_ref[...].astype(BF16), wpp_ref[...])
    x2 = x1 + emb * _sigmoid(gate)
    xo_ref[...] = x2
    if next_refs:
        normed = x2 * _rms_scale(x2)
        for n, ref in enumerate(next_refs):
            ref[...] = (normed * gnext_ref[n:n + 1, :]).astype(BF16)


def _out_proj(og2d, x2d, p2d, w_out, g_post, w_pp, w_pg, next_gains):
    t, d = x2d.shape
    width = og2d.shape[1]
    ple = p2d.shape[1]
    tm = _tile(t, 512)
    n_next = len(next_gains)
    gnext = jnp.stack(next_gains) if n_next else jnp.zeros((1, d), F32)
    row = lambda i: (i, 0)
    outs = pl.pallas_call(
        _out_kernel,
        out_shape=[jax.ShapeDtypeStruct((t, d), F32)]
        + [jax.ShapeDtypeStruct((t, d), BF16)] * n_next,
        grid=(t // tm,),
        in_specs=[
            pl.BlockSpec((tm, width), row),
            pl.BlockSpec((tm, d), row),
            pl.BlockSpec((tm, ple), row),
            _resident((width, d)),
            _resident((1, d)),
            _resident((ple, d)),
            _resident((d, d)),
            _resident(gnext.shape),
        ],
        out_specs=[pl.BlockSpec((tm, d), row)] * (1 + n_next),
        compiler_params=_params("parallel"),
        name="out_proj",
    )(og2d, x2d, p2d, w_out, g_post.reshape(1, d), w_pp, w_pg, gnext)
    return outs[0], list(outs[1:])


def _kv_kernel(h_ref, wd_ref, gl_ref, wuk_ref, wuvt_ref, cos_ref, sin_ref, k_ref, vt_ref,
               *, rank, heads):
    c = _dot(h_ref[...], wd_ref[...])
    ckv = c[:, :rank]
    k_rope = c[:, rank:rank + ROPE_DIM]
    k_rot = c[:, rank + 128:rank + 128 + ROPE_DIM]
    latent = (ckv * _rms_scale(ckv) * gl_ref[...]).astype(BF16)
    k_nope = _dot(latent, wuk_ref[...]).astype(BF16)
    vt_ref[...] = _dot_nt(wuvt_ref[...], latent).astype(BF16)
    roped = (k_rope * cos_ref[...] + k_rot * sin_ref[...]).astype(BF16)
    for h in range(heads):
        k_ref[h, :, 0:HEAD_DIM] = k_nope[:, h * HEAD_DIM:(h + 1) * HEAD_DIM]
        k_ref[h, :, HEAD_DIM:QK_DIM] = roped


def _kv_side(hkv, wd, g_latent, w_uk, w_uv_t, cos, sin, heads):
    b, s, d = hkv.shape
    rank = w_uk.shape[0]
    tm = _tile(s, 512)
    row = lambda bi, i: (bi, i, 0)
    return pl.pallas_call(
        functools.partial(_kv_kernel, rank=rank, heads=heads),
        out_shape=[jax.ShapeDtypeStruct((b, heads, s, QK_DIM), BF16),
                   jax.ShapeDtypeStruct((b, heads * HEAD_DIM, s), BF16)],
        grid=(b, s // tm),
        in_specs=[
            pl.BlockSpec((None, tm, d), row),
            _resident(wd.shape),
            _resident((1, rank)),
            _resident(w_uk.shape),
            _resident(w_uv_t.shape),
            pl.BlockSpec((tm, ROPE_DIM), lambda bi, i: (i, 0)),
            pl.BlockSpec((tm, ROPE_DIM), lambda bi, i: (i, 0)),
        ],
        out_specs=[pl.BlockSpec((None, heads, tm, QK_DIM), lambda bi, i: (bi, 0, i, 0)),
                   pl.BlockSpec((None, heads * HEAD_DIM, tm), lambda bi, i: (bi, 0, i))],
        compiler_params=_params("parallel", "parallel"),
        name="kv_side",
    )(hkv, wd, g_latent.reshape(1, rank), w_uk, w_uv_t, cos, sin)


def _q_kernel(h_ref, win_ref, gq_ref, wnt_ref, wrt_ref, wrott_ref, cost_ref, sint_ref,
              qt_ref, sg_ref, *, rank, heads, scale):
    c = _dot(h_ref[...], win_ref[...])
    cq = c[:, :rank]
    gate = c[:, rank:]
    sg_ref[...] = (gate * _sigmoid(gate)).astype(BF16)
    latent = (cq * _rms_scale(cq) * gq_ref[...]).astype(BF16)
    nope_t = (_dot_nt(wnt_ref[...], latent) * scale).astype(BF16)
    cos_t = jnp.tile(cost_ref[...], (heads, 1))
    sin_t = jnp.tile(sint_ref[...], (heads, 1))
    roped_t = ((_dot_nt(wrt_ref[...], latent) * cos_t + _dot_nt(wrott_ref[...], latent) * sin_t)
               * scale).astype(BF16)
    for h in range(heads):
        qt_ref[h, 0:HEAD_DIM, :] = nope_t[h * HEAD_DIM:(h + 1) * HEAD_DIM, :]
        qt_ref[h, HEAD_DIM:QK_DIM, :] = roped_t[h * ROPE_DIM:(h + 1) * ROPE_DIM, :]


def _q_side(h, w_in, g_latent, w_n_t, w_r_t, w_rot_t, cos_t, sin_t, heads):
    b, s, d = h.shape
    rank = w_n_t.shape[1]
    width = w_in.shape[1] - rank
    tm = _tile(s, 512)
    row = lambda bi, i: (bi, i, 0)
    return pl.pallas_call(
        functools.partial(_q_kernel, rank=rank, heads=heads, scale=QK_DIM ** -0.5 * LOG2_E),
        out_shape=[jax.ShapeDtypeStruct((b, heads, QK_DIM, s), BF16),
                   jax.ShapeDtypeStruct((b, s, width), BF16)],
        grid=(b, s // tm),
        in_specs=[
            pl.BlockSpec((None, tm, d), row),
            _resident(w_in.shape),
            _resident((1, rank)),
            _resident(w_n_t.shape),
            _resident(w_r_t.shape),
            _resident(w_rot_t.shape),
            pl.BlockSpec((ROPE_DIM, tm), lambda bi, i: (0, i)),
            pl.BlockSpec((ROPE_DIM, tm), lambda bi, i: (0, i)),
        ],
        out_specs=[pl.BlockSpec((None, heads, QK_DIM, tm), lambda bi, i: (bi, 0, 0, i)),
                   pl.BlockSpec((None, tm, width), row)],
        compiler_params=_params("parallel", "parallel"),
        name="q_side",
    )(h, w_in, g_latent.reshape(1, rank), w_n_t, w_r_t, w_rot_t, cos_t, sin_t)


def _mla_attn_kernel(qt_ref, k_ref, vt_ref, sg_ref, o_ref, s_scr, p_scr, acc_scr, *, tq, tk):
    for hh in range(HEADS_PER_STEP):
        cols = pl.ds(hh * HEAD_DIM, HEAD_DIM)
        _mla_head(qt_ref.at[hh], k_ref.at[hh], vt_ref.at[cols, :], sg_ref.at[:, cols],
                  o_ref.at[:, cols], s_scr, p_scr, acc_scr, tq=tq, tk=tk)


def _mla_head(qt_ref, k_ref, vt_ref, sg_ref, o_ref, s_scr, p_scr, acc_scr, *, tq, tk):
    i = pl.program_id(2)
    nblk = (i + 1) * (tq // tk)
    q_t = qt_ref[...]
    acc_scr[...] = jnp.zeros_like(acc_scr)

    def key_start(n):
        return pl.multiple_of(n * tk, tk)

    def step(n, parity, stats, run, *, merge, softmax, score, mask=None):
        if score:
            s_scr[parity] = _dot(k_ref[pl.ds(key_start(n), tk), :], q_t)
        if merge:
            out = _dot(vt_ref[:, pl.ds(key_start(n - 2), tk)], p_scr[parity])
        m_c, l_c = stats
        if softmax:
            s = s_scr[1 - parity]
            if mask is not None:
                s = jnp.where(mask, s, NEG_INF)
            m_next = jnp.max(s, axis=0, keepdims=True)
            prob = jnp.exp2(s - m_next)
            p_scr[1 - parity] = prob.astype(BF16)
            stats = (m_next, jnp.sum(prob, axis=0, keepdims=True))
        if merge:
            m_run, l_run = run
            m_new = jnp.maximum(m_run, m_c)
            alpha = jnp.exp2(m_run - m_new)
            weight = jnp.exp2(m_c - m_new)
            acc_scr[...] = acc_scr[...] * alpha + out * weight
            run = (m_new, l_run * alpha + l_c * weight)
        return stats, run

    def finish(run):
        out = (acc_scr[...] / run[1]).T
        o_ref[...] = (out * sg_ref[...].astype(F32)).astype(BF16)

    kchunk = lax.broadcasted_iota(jnp.int32, (tk, tq), 0) // CHUNK
    qchunk = lax.broadcasted_iota(jnp.int32, (tk, tq), 1) // CHUNK
    first_diag = kchunk <= qchunk
    second_diag = (kchunk + tk // CHUNK) <= qchunk
    row = jnp.zeros((1, tq), F32)
    state = ((row, row), (jnp.full((1, tq), NEG_INF, F32), row))
    everything = dict(merge=True, softmax=True, score=True)

    @pl.when(i == 0)
    def _():
        st = step(0, 0, *state, merge=False, softmax=False, score=True)
        st = step(1, 1, *st, merge=False, softmax=True, score=True, mask=first_diag)
        st = step(2, 0, *st, merge=True, softmax=True, score=False, mask=second_diag)
        st = step(3, 1, *st, merge=True, softmax=False, score=False)
        finish(st[1])

    @pl.when(i > 0)
    def _():
        st = step(0, 0, *state, merge=False, softmax=False, score=True)
        st = step(1, 1, *st, merge=False, softmax=True, score=True)
        st = step(2, 0, *st, **everything)

        def pair(t, st):
            st = step(3 + 2 * t, 1, *st, **everything)
            return step(4 + 2 * t, 0, *st, **everything)

        st = lax.fori_loop(0, (nblk - 4) // 2, pair, st)
        st = step(nblk - 1, 1, *st, mask=first_diag, **everything)
        st = step(nblk, 0, *st, merge=True, softmax=True, score=False, mask=second_diag)
        st = step(nblk + 1, 1, *st, merge=True, softmax=False, score=False)
        finish(st[1])


def _mla_attention(q_t, k_cat, v_t, sg):
    b, heads, _, s = q_t.shape
    tq = _tile(s, 512)
    tk = _tile(s, 256)
    assert tq == 2 * tk, "the pipeline below peels exactly two diagonal key blocks"
    width = HEADS_PER_STEP * HEAD_DIM
    return pl.pallas_call(
        functools.partial(_mla_attn_kernel, tq=tq, tk=tk),
        out_shape=jax.ShapeDtypeStruct((b, s, heads * HEAD_DIM), BF16),
        grid=(b, heads // HEADS_PER_STEP, s // tq),
        in_specs=[
            pl.BlockSpec((None, HEADS_PER_STEP, QK_DIM, tq), lambda bi, h, i: (bi, h, 0, i)),
            pl.BlockSpec((None, HEADS_PER_STEP, s, QK_DIM), lambda bi, h, i: (bi, h, 0, 0)),
            pl.BlockSpec((None, width, s), lambda bi, h, i: (bi, h, 0)),
            pl.BlockSpec((None, tq, width), lambda bi, h, i: (bi, i, h)),
        ],
        out_specs=pl.BlockSpec((None, tq, width), lambda bi, h, i: (bi, i, h)),
        scratch_shapes=[pltpu.VMEM((2, tk, tq), F32), pltpu.VMEM((2, tk, tq), BF16),
                        pltpu.VMEM((HEAD_DIM, tq), F32)],
        compiler_params=_params("parallel", "parallel", "arbitrary"),
        name="mla_attn",
    )(q_t, k_cat, v_t, sg)


def _rotate_half_columns(w):
    half = ROPE_DIM // 2
    return jnp.concatenate([-w[..., half:], w[..., :half]], axis=-1)


def _rope_tables(s):
    inv = 1.0 / (ROPE_THETA ** (jnp.arange(0, ROPE_DIM, 2, dtype=F32) / ROPE_DIM))
    ang = jnp.arange(s, dtype=F32)[:, None] * inv[None, :]
    cos, sin = jnp.cos(ang), jnp.sin(ang)
    return jnp.tile(cos, (1, 2)), jnp.tile(sin, (1, 2))


def kernel(x, p, norm_pre, norm_post, w_in_a, w_out_a, w_in_b, q_latent_norm, w_uq, w_out_b,
           kv_norm, w_dkv, kv_latent_norm, w_uk, w_uv, w_ple_proj, w_ple_gate):
    b, s, d = x.shape
    t = b * s
    n_a, n_b = w_in_a.shape[0], w_in_b.shape[0]
    heads = d // HEAD_DIM
    q_rank = w_uq.shape[1]
    kv_rank = w_uk.shape[0]
    cos, sin = _rope_tables(s)

    x2d = x.reshape(t, d)
    p2d = p.reshape(p.shape[0], t, p.shape[-1])

    def finish_layer(layer, og, x2d, w_out):
        if layer + 1 < n_a:
            gains = [norm_pre[layer + 1]]
        elif layer + 1 == n_a and n_b:
            gains = [norm_pre[layer + 1], kv_norm]
        elif layer + 1 < n_a + n_b:
            gains = [norm_pre[layer + 1]]
        else:
            gains = []
        return _out_proj(og.reshape(t, -1), x2d, p2d[layer], w_out.astype(BF16),
                         norm_post[layer], w_ple_proj[layer].astype(BF16),
                         w_ple_gate[layer].astype(BF16), gains)

    h_next = None
    for i in range(n_a):
        qkvg = _proj_a(x2d, norm_pre[i], w_in_a[i])
        og = _sb_attention(qkvg.reshape(b, s, -1), heads)
        x2d, h_next = finish_layer(i, og, x2d, w_out_a[i])

    if n_b == 0:
        return x2d.reshape(b, s, d)
    if n_a == 0:
        raise NotImplementedError("a trunk without mixer-A layers is not supported")

    h_b, h_kv = h_next
    wd = jnp.concatenate([
        w_dkv[:, :kv_rank],
        jnp.pad(w_dkv[:, kv_rank:], ((0, 0), (0, 128 - ROPE_DIM))),
        jnp.pad(_rotate_half_columns(w_dkv[:, kv_rank:]), ((0, 0), (0, 128 - ROPE_DIM))),
    ], axis=1).astype(BF16)
    k_cat, v_t = _kv_side(h_kv.reshape(b, s, d), wd, kv_latent_norm, w_uk.astype(BF16),
                          w_uv.T.astype(BF16), cos, sin, heads)

    for j in range(n_b):
        layer = n_a + j
        wq = w_uq[j].reshape(q_rank, heads, QK_DIM)
        w_n_t = wq[:, :, :HEAD_DIM].reshape(q_rank, heads * HEAD_DIM).T.astype(BF16)
        w_r_t = wq[:, :, HEAD_DIM:].reshape(q_rank, heads * ROPE_DIM).T.astype(BF16)
        w_rot_t = _rotate_half_columns(wq[:, :, HEAD_DIM:]).reshape(
            q_rank, heads * ROPE_DIM).T.astype(BF16)
        q_t, sg = _q_side(h_b.reshape(b, s, d), w_in_b[j].astype(BF16), q_latent_norm[j],
                          w_n_t, w_r_t, w_rot_t, cos.T, sin.T, heads)
        og = _mla_attention(q_t, k_cat, v_t, sg)
        x2d, h_next = finish_layer(layer, og, x2d, w_out_b[j])
        if h_next:
            h_b = h_next[0]
    return x2d.reshape(b, s, d)
```

```python
import functools
import math

import jax
import jax.numpy as jnp
from jax import lax
from jax.experimental import pallas as pl
from jax.experimental.pallas import tpu as pltpu

RMS_EPS = 1e-6
NEG_INF = -1e30
HEAD_DIM = 128
ROPE_DIM = 64
QK_DIM = HEAD_DIM + ROPE_DIM
CHUNK = 64
HEADS_PER_STEP = 2
ROPE_THETA = 10000.0
LOG2_E = math.log2(math.e)
EXP2_UNDERFLOW = -160.0

V7X_VMEM_LIMIT_BYTES = 56 * 1024 * 1024
BF16 = jnp.bfloat16
F32 = jnp.float32


def _tile(n, pref):
    t = min(n, pref)
    while n % t:
        t //= 2
    return t


def _params(*sem):
    return pltpu.CompilerParams(dimension_semantics=sem,
                                vmem_limit_bytes=V7X_VMEM_LIMIT_BYTES)


def _resident(shape):
    nd = len(shape)
    return pl.BlockSpec(shape, lambda *_: (0,) * nd, pipeline_mode=pl.Buffered(1))


def _rms_scale(v):
    return lax.rsqrt(jnp.mean(v * v, axis=-1, keepdims=True) + RMS_EPS)


def _sigmoid(v):
    return 1.0 / (1.0 + jnp.exp(-v))


def _dot(a, b):
    return jnp.dot(a, b, preferred_element_type=F32)


def _dot_nt(a, b):
    return lax.dot_general(a, b, (((1,), (1,)), ((), ())), preferred_element_type=F32)


def _proj_a_kernel(x_ref, g_ref, w_ref, o_ref, h_scr, *, q_blocks, q_scale):
    j = pl.program_id(1)

    @pl.when(j == 0)
    def _():
        x = x_ref[...]
        h_scr[...] = (x * _rms_scale(x) * g_ref[...]).astype(BF16)

    acc = _dot(h_scr[...], w_ref[...].astype(BF16))

    @pl.when(j < q_blocks)
    def _():
        o_ref[...] = (acc * q_scale).astype(BF16)

    @pl.when((j >= q_blocks) & (j < 3 * q_blocks))
    def _():
        o_ref[...] = acc.astype(BF16)

    @pl.when(j >= 3 * q_blocks)
    def _():
        o_ref[...] = (acc * _sigmoid(acc)).astype(BF16)


def _proj_a(x2d, gain, w_in):
    t, d = x2d.shape
    n = w_in.shape[1]
    width = n // 4
    tm = _tile(t, 1024)
    tn = _tile(width, 1024)
    return pl.pallas_call(
        functools.partial(_proj_a_kernel, q_blocks=width // tn,
                          q_scale=HEAD_DIM ** -0.5 * LOG2_E),
        out_shape=jax.ShapeDtypeStruct((t, n), BF16),
        grid=(t // tm, n // tn),
        in_specs=[
            pl.BlockSpec((tm, d), lambda i, j: (i, 0)),
            pl.BlockSpec((1, d), lambda i, j: (0, 0)),
            pl.BlockSpec((d, tn), lambda i, j: (0, j)),
        ],
        out_specs=pl.BlockSpec((tm, tn), lambda i, j: (i, j)),
        scratch_shapes=[pltpu.VMEM((tm, d), BF16)],
        compiler_params=_params("parallel", "arbitrary"),
        name="proj_a",
    )(x2d, gain.reshape(1, d), w_in)


def _sb_attn_kernel(q_ref, k_ref, v_ref, sg_ref, upper_ref, o_ref, vt_scr, acc_scr, carry_scr,
                    *, tq, tk):
    i = pl.program_id(2)
    heads = []
    for hh in range(HEADS_PER_STEP):
        cols = pl.ds(hh * HEAD_DIM, HEAD_DIM)
        heads.append(functools.partial(
            _sb_head, q_ref.at[:, cols], k_ref.at[:, cols], v_ref.at[:, cols], sg_ref.at[:, cols],
            upper_ref, o_ref.at[:, cols], vt_scr.at[hh], acc_scr.at[hh], carry_scr.at[hh],
            i=i, tq=tq, tk=tk))

    @pl.when(i == 0)
    def _():
        for head in heads:
            head(first_step=True)

    @pl.when(i > 0)
    def _():
        for head in heads:
            head(first_step=False)


def _sb_head(q_ref, k_ref, v_ref, sg_ref, upper_ref, o_ref, vt_scr, acc_scr, carry_scr,
             *, i, tq, tk, first_step):
    if first_step:
        for c in range(vt_scr.shape[0]):
            vt_scr[c] = v_ref[c * tk:(c + 1) * tk, :].T

    q_t = q_ref[...].T
    upper = upper_ref[...]

    def scores(blk, half):
        start = pl.multiple_of(blk * tk, tk)
        return _dot(k_ref[pl.ds(start, tk), :], q_t[:, half * tk:(half + 1) * tk])

    def stage(z, strict):
        soft = jnp.log2(1.0 + jnp.exp2(-jnp.abs(z)))
        log_beta = jnp.minimum(z, 0.0) - soft
        log_keep = log_beta - z
        if strict is not None:
            log_keep = jnp.where(strict, log_keep, 0.0)
        later = _dot(upper, log_keep.astype(BF16))
        return log_beta + later, jnp.sum(log_keep, axis=0, keepdims=True)

    def group(chains, carry, acc):
        raw = [scores(blk, half) for blk, _, half, _ in chains]
        staged = [stage(z, strict) for z, (_, strict, _, _) in zip(raw, chains)]
        carry, acc = list(carry), list(acc)
        for (blk, strict, half, valid), (pre, total) in zip(chains, staged):
            w = jnp.exp2(pre + carry[half])
            if strict is not None:
                w = jnp.where(strict, w, 0.0)
            if valid is not None:
                w = jnp.where(valid, w, 0.0)
                total = jnp.where(valid, total, 0.0)
            acc[half] = acc[half] + _dot(vt_scr[blk], w.astype(BF16))
            carry[half] = carry[half] + total
        acc_scr[...] = jnp.concatenate(acc, axis=1)
        carry_scr[...] = jnp.concatenate(carry, axis=1)

    key = lax.broadcasted_iota(jnp.int32, (tk, tk), 0)
    qry = lax.broadcasted_iota(jnp.int32, (tk, tk), 1)
    strict = key < qry
    first = 2 * i
    zeros = ([jnp.zeros((1, tk), F32)] * 2, [jnp.zeros((HEAD_DIM, tk), F32)] * 2)

    if first_step:
        group([(1, strict, 1, None), (0, strict, 0, None), (0, None, 1, None)], *zeros)
    else:
        group([(first + 1, strict, 1, None), (first, strict, 0, None),
               (first, None, 1, None), (first - 1, None, 0, None)], *zeros)

    def live(blk):
        return (blk >= 0) & (jnp.max(carry_scr[...]) > EXP2_UNDERFLOW)

    def body(blk):
        carry, acc = carry_scr[...], acc_scr[...]
        group([(blk, None, 1, None), (jnp.maximum(blk - 1, 0), None, 0, blk >= 1)],
              [carry[:, :tk], carry[:, tk:]], [acc[:, :tk], acc[:, tk:]])
        return blk - 1

    lax.while_loop(live, body, first - 1)
    o_ref[...] = (acc_scr[...].T * sg_ref[...].astype(F32)).astype(BF16)


def _sb_attention(qkvg, heads):
    b, s, _ = qkvg.shape
    tq = _tile(s, 512)
    tk = _tile(s, 256)
    assert tq == 2 * tk, "the kernel walks the query block as two halves of tk queries"
    groups, width = heads // HEADS_PER_STEP, HEADS_PER_STEP * HEAD_DIM
    upper = (lax.broadcasted_iota(jnp.int32, (tk, tk), 1)
             > lax.broadcasted_iota(jnp.int32, (tk, tk), 0)).astype(BF16)
    return pl.pallas_call(
        functools.partial(_sb_attn_kernel, tq=tq, tk=tk),
        out_shape=jax.ShapeDtypeStruct((b, s, heads * HEAD_DIM), BF16),
        grid=(b, groups, s // tq),
        in_specs=[
            pl.BlockSpec((None, tq, width), lambda bi, h, i: (bi, i, h)),
            pl.BlockSpec((None, s, width), lambda bi, h, i: (bi, 0, groups + h)),
            pl.BlockSpec((None, s, width), lambda bi, h, i: (bi, 0, 2 * groups + h)),
            pl.BlockSpec((None, tq, width), lambda bi, h, i: (bi, i, 3 * groups + h)),
            pl.BlockSpec((tk, tk), lambda bi, h, i: (0, 0)),
        ],
        out_specs=pl.BlockSpec((None, tq, width), lambda bi, h, i: (bi, i, h)),
        scratch_shapes=[pltpu.VMEM((HEADS_PER_STEP, s // tk, HEAD_DIM, tk), BF16),
                        pltpu.VMEM((HEADS_PER_STEP, HEAD_DIM, tq), F32),
                        pltpu.VMEM((HEADS_PER_STEP, 1, tq), F32)],
        compiler_params=_params("parallel", "parallel", "arbitrary"),
        name="sb_attn",
    )(qkvg, qkvg, qkvg, qkvg, upper)


def _out_kernel(og_ref, x_ref, p_ref, wout_ref, gpost_ref, wpp_ref, wpg_ref, gnext_ref,
                xo_ref, *next_refs):
    y = _dot(og_ref[...], wout_ref[...])
    x1 = x_ref[...] + y * _rms_scale(y) * gpost_ref[...]
    gate = _dot(x1.astype(BF16), wpg_ref[...])
    emb = _dot(p_ref[...].astype(BF16), wpp_ref[...])
    x2 = x1 + emb * _sigmoid(gate)
    xo_ref[...] = x2
    if next_refs:
        normed = x2 * _rms_scale(x2)
        for n, ref in enumerate(next_refs):
            ref[...] = (normed * gnext_ref[n:n + 1, :]).astype(BF16)


def _out_proj(og2d, x2d, p2d, w_out, g_post, w_pp, w_pg, next_gains):
    t, d = x2d.shape
    width = og2d.shape[1]
    ple = p2d.shape[1]
    tm = _tile(t, 512)
    n_next = len(next_gains)
    gnext = jnp.stack(next_gains) if n_next else jnp.zeros((1, d), F32)
    row = lambda i: (i, 0)
    outs = pl.pallas_call(
        _out_kernel,
        out_shape=[jax.ShapeDtypeStruct((t, d), F32)]
        + [jax.ShapeDtypeStruct((t, d), BF16)] * n_next,
        grid=(t // tm,),
        in_specs=[
            pl.BlockSpec((tm, width), row),
            pl.BlockSpec((tm, d), row),
            pl.BlockSpec((tm, ple), row),
            _resident((width, d)),
            _resident((1, d)),
            _resident((ple, d)),
            _resident((d, d)),
            _resident(gnext.shape),
        ],
        out_specs=[pl.BlockSpec((tm, d), row)] * (1 + n_next),
        compiler_params=_params("parallel"),
        name="out_proj",
    )(og2d, x2d, p2d, w_out, g_post.reshape(1, d), w_pp, w_pg, gnext)
    return outs[0], list(outs[1:])


def _kv_kernel(h_ref, wd_ref, gl_ref, wuk_ref, wuvt_ref, cos_ref, sin_ref, k_ref, vt_ref,
               *, rank, heads):
    c = _dot(h_ref[...], wd_ref[...])
    ckv = c[:, :rank]
    k_rope = c[:, rank:rank + ROPE_DIM]
    k_rot = c[:, rank + 128:rank + 128 + ROPE_DIM]
    latent = (ckv * _rms_scale(ckv) * gl_ref[...]).astype(BF16)
    k_nope = _dot(latent, wuk_ref[...]).astype(BF16)
    vt_ref[...] = _dot_nt(wuvt_ref[...], latent).astype(BF16)
    roped = (k_rope * cos_ref[...] + k_rot * sin_ref[...]).astype(BF16)
    for h in range(heads):
        k_ref[h, :, 0:HEAD_DIM] = k_nope[:, h * HEAD_DIM:(h + 1) * HEAD_DIM]
        k_ref[h, :, HEAD_DIM:QK_DIM] = roped


def _kv_side(hkv, wd, g_latent, w_uk, w_uv_t, cos, sin, heads):
    b, s, d = hkv.shape
    rank = w_uk.shape[0]
    tm = _tile(s, 512)
    row = lambda bi, i: (bi, i, 0)
    return pl.pallas_call(
        functools.partial(_kv_kernel, rank=rank, heads=heads),
        out_shape=[jax.ShapeDtypeStruct((b, heads, s, QK_DIM), BF16),
                   jax.ShapeDtypeStruct((b, heads * HEAD_DIM, s), BF16)],
        grid=(b, s // tm),
        in_specs=[
            pl.BlockSpec((None, tm, d), row),
            _resident(wd.shape),
            _resident((1, rank)),
            _resident(w_uk.shape),
            _resident(w_uv_t.shape),
            pl.BlockSpec((tm, ROPE_DIM), lambda bi, i: (i, 0)),
            pl.BlockSpec((tm, ROPE_DIM), lambda bi, i: (i, 0)),
        ],
        out_specs=[pl.BlockSpec((None, heads, tm, QK_DIM), lambda bi, i: (bi, 0, i, 0)),
                   pl.BlockSpec((None, heads * HEAD_DIM, tm), lambda bi, i: (bi, 0, i))],
        compiler_params=_params("parallel", "parallel"),
        name="kv_side",
    )(hkv, wd, g_latent.reshape(1, rank), w_uk, w_uv_t, cos, sin)


def _q_kernel(h_ref, win_ref, gq_ref, wnt_ref, wrt_ref, wrott_ref, cost_ref, sint_ref,
              qt_ref, sg_ref, *, rank, heads, scale):
    c = _dot(h_ref[...], win_ref[...])
    cq = c[:, :rank]
    gate = c[:, rank:]
    sg_ref[...] = (gate * _sigmoid(gate)).astype(BF16)
    latent = (cq * _rms_scale(cq) * gq_ref[...]).astype(BF16)
    nope_t = (_dot_nt(wnt_ref[...], latent) * scale).astype(BF16)
    cos_t = jnp.tile(cost_ref[...], (heads, 1))
    sin_t = jnp.tile(sint_ref[...], (heads, 1))
    roped_t = ((_dot_nt(wrt_ref[...], latent) * cos_t + _dot_nt(wrott_ref[...], latent) * sin_t)
               * scale).astype(BF16)
    for h in range(heads):
        qt_ref[h, 0:HEAD_DIM, :] = nope_t[h * HEAD_DIM:(h + 1) * HEAD_DIM, :]
        qt_ref[h, HEAD_DIM:QK_DIM, :] = roped_t[h * ROPE_DIM:(h + 1) * ROPE_DIM, :]


def _q_side(h, w_in, g_latent, w_n_t, w_r_t, w_rot_t, cos_t, sin_t, heads):
    b, s, d = h.shape
    rank = w_n_t.shape[1]
    width = w_in.shape[1] - rank
    tm = _tile(s, 512)
    row = lambda bi, i: (bi, i, 0)
    return pl.pallas_call(
        functools.partial(_q_kernel, rank=rank, heads=heads, scale=QK_DIM ** -0.5 * LOG2_E),
        out_shape=[jax.ShapeDtypeStruct((b, heads, QK_DIM, s), BF16),
                   jax.ShapeDtypeStruct((b, s, width), BF16)],
        grid=(b, s // tm),
        in_specs=[
            pl.BlockSpec((None, tm, d), row),
            _resident(w_in.shape),
            _resident((1, rank)),
            _resident(w_n_t.shape),
            _resident(w_r_t.shape),
            _resident(w_rot_t.shape),
            pl.BlockSpec((ROPE_DIM, tm), lambda bi, i: (0, i)),
            pl.BlockSpec((ROPE_DIM, tm), lambda bi, i: (0, i)),
        ],
        out_specs=[pl.BlockSpec((None, heads, QK_DIM, tm), lambda bi, i: (bi, 0, 0, i)),
                   pl.BlockSpec((None, tm, width), row)],
        compiler_params=_params("parallel", "parallel"),
        name="q_side",
    )(h, w_in, g_latent.reshape(1, rank), w_n_t, w_r_t, w_rot_t, cos_t, sin_t)


def _mla_attn_kernel(qt_ref, k_ref, vt_ref, sg_ref, o_ref, s_scr, p_scr, acc_scr, *, tq, tk):
    i = pl.program_id(2)
    heads = []
    for hh in range(HEADS_PER_STEP):
        cols = pl.ds(hh * HEAD_DIM, HEAD_DIM)
        heads.append(functools.partial(
            _mla_head, qt_ref.at[hh], k_ref.at[hh], vt_ref.at[cols, :], sg_ref.at[:, cols],
            o_ref.at[:, cols], s_scr.at[hh], p_scr.at[hh], acc_scr.at[hh], i=i, tq=tq, tk=tk))

    @pl.when(i == 0)
    def _():
        for head in heads:
            head(first_step=True)

    @pl.when(i > 0)
    def _():
        for head in heads:
            head(first_step=False)


def _mla_head(qt_ref, k_ref, vt_ref, sg_ref, o_ref, s_scr, p_scr, acc_scr,
              *, i, tq, tk, first_step):
    nblk = (i + 1) * (tq // tk)
    q_t = qt_ref[...]
    acc_scr[...] = jnp.zeros_like(acc_scr)

    def key_start(n):
        return pl.multiple_of(n * tk, tk)

    def step(n, parity, stats, run, *, merge, softmax, score, mask=None):
        if score:
            s_scr[parity] = _dot(k_ref[pl.ds(key_start(n), tk), :], q_t)
        if merge:
            out = _dot(vt_ref[:, pl.ds(key_start(n - 2), tk)], p_scr[parity])
        m_c, l_c = stats
        if softmax:
            s = s_scr[1 - parity]
            if mask is not None:
                s = jnp.where(mask, s, NEG_INF)
            m_next = jnp.max(s, axis=0, keepdims=True)
            prob = jnp.exp2(s - m_next)
            p_scr[1 - parity] = prob.astype(BF16)
            stats = (m_next, jnp.sum(prob, axis=0, keepdims=True))
        if merge:
            m_run, l_run = run
            m_new = jnp.maximum(m_run, m_c)
            alpha = jnp.exp2(m_run - m_new)
            weight = jnp.exp2(m_c - m_new)
            acc_scr[...] = acc_scr[...] * alpha + out * weight
            run = (m_new, l_run * alpha + l_c * weight)
        return stats, run

    def finish(run):
        out = (acc_scr[...] / run[1]).T
        o_ref[...] = (out * sg_ref[...].astype(F32)).astype(BF16)

    kchunk = lax.broadcasted_iota(jnp.int32, (tk, tq), 0) // CHUNK
    qchunk = lax.broadcasted_iota(jnp.int32, (tk, tq), 1) // CHUNK
    first_diag = kchunk <= qchunk
    second_diag = (kchunk + tk // CHUNK) <= qchunk
    row = jnp.zeros((1, tq), F32)
    state = ((row, row), (jnp.full((1, tq), NEG_INF, F32), row))
    everything = dict(merge=True, softmax=True, score=True)

    if first_step:
        st = step(0, 0, *state, merge=False, softmax=False, score=True)
        st = step(1, 1, *st, merge=False, softmax=True, score=True, mask=first_diag)
        st = step(2, 0, *st, merge=True, softmax=True, score=False, mask=second_diag)
        st = step(3, 1, *st, merge=True, softmax=False, score=False)
        finish(st[1])
    else:
        st = step(0, 0, *state, merge=False, softmax=False, score=True)
        st = step(1, 1, *st, merge=False, softmax=True, score=True)
        st = step(2, 0, *st, **everything)

        def pair(t, st):
            st = step(3 + 2 * t, 1, *st, **everything)
            return step(4 + 2 * t, 0, *st, **everything)

        st = lax.fori_loop(0, (nblk - 4) // 2, pair, st)
        st = step(nblk - 1, 1, *st, mask=first_diag, **everything)
        st = step(nblk, 0, *st, merge=True, softmax=True, score=False, mask=second_diag)
        st = step(nblk + 1, 1, *st, merge=True, softmax=False, score=False)
        finish(st[1])


def _mla_attention(q_t, k_cat, v_t, sg):
    b, heads, _, s = q_t.shape
    tq = _tile(s, 512)
    tk = _tile(s, 256)
    assert tq == 2 * tk, "the pipeline below peels exactly two diagonal key blocks"
    width = HEADS_PER_STEP * HEAD_DIM
    return pl.pallas_call(
        functools.partial(_mla_attn_kernel, tq=tq, tk=tk),
        out_shape=jax.ShapeDtypeStruct((b, s, heads * HEAD_DIM), BF16),
        grid=(b, heads // HEADS_PER_STEP, s // tq),
        in_specs=[
            pl.BlockSpec((None, HEADS_PER_STEP, QK_DIM, tq), lambda bi, h, i: (bi, h, 0, i)),
            pl.BlockSpec((None, HEADS_PER_STEP, s, QK_DIM), lambda bi, h, i: (bi, h, 0, 0)),
            pl.BlockSpec((None, width, s), lambda bi, h, i: (bi, h, 0)),
            pl.BlockSpec((None, tq, width), lambda bi, h, i: (bi, i, h)),
        ],
        out_specs=pl.BlockSpec((None, tq, width), lambda bi, h, i: (bi, i, h)),
        scratch_shapes=[pltpu.VMEM((HEADS_PER_STEP, 2, tk, tq), F32),
                        pltpu.VMEM((HEADS_PER_STEP, 2, tk, tq), BF16),
                        pltpu.VMEM((HEADS_PER_STEP, HEAD_DIM, tq), F32)],
        compiler_params=_params("parallel", "parallel", "arbitrary"),
        name="mla_attn",
    )(q_t, k_cat, v_t, sg)


def _rotate_half_columns(w):
    half = ROPE_DIM // 2
    return jnp.concatenate([-w[..., half:], w[..., :half]], axis=-1)


def _rope_tables(s):
    inv = 1.0 / (ROPE_THETA ** (jnp.arange(0, ROPE_DIM, 2, dtype=F32) / ROPE_DIM))
    ang = jnp.arange(s, dtype=F32)[:, None] * inv[None, :]
    cos, sin = jnp.cos(ang), jnp.sin(ang)
    return jnp.tile(cos, (1, 2)), jnp.tile(sin, (1, 2))


def kernel(x, p, norm_pre, norm_post, w_in_a, w_out_a, w_in_b, q_latent_norm, w_uq, w_out_b,
           kv_norm, w_dkv, kv_latent_norm, w_uk, w_uv, w_ple_proj, w_ple_gate):
    b, s, d = x.shape
    t = b * s
    n_a, n_b = w_in_a.shape[0], w_in_b.shape[0]
    heads = d // HEAD_DIM
    q_rank = w_uq.shape[1]
    kv_rank = w_uk.shape[0]
    cos, sin = _rope_tables(s)

    x2d = x.reshape(t, d)
    p2d = p.reshape(p.shape[0], t, p.shape[-1])

    def finish_layer(layer, og, x2d, w_out):
        if layer + 1 < n_a:
            gains = [norm_pre[layer + 1]]
        elif layer + 1 == n_a and n_b:
            gains = [norm_pre[layer + 1], kv_norm]
        elif layer + 1 < n_a + n_b:
            gains = [norm_pre[layer + 1]]
        else:
            gains = []
        return _out_proj(og.reshape(t, -1), x2d, p2d[layer], w_out.astype(BF16),
                         norm_post[layer], w_ple_proj[layer].astype(BF16),
                         w_ple_gate[layer].astype(BF16), gains)

    h_next = None
    for i in range(n_a):
        qkvg = _proj_a(x2d, norm_pre[i], w_in_a[i])
        og = _sb_attention(qkvg.reshape(b, s, -1), heads)
        x2d, h_next = finish_layer(i, og, x2d, w_out_a[i])

    if n_b == 0:
        return x2d.reshape(b, s, d)
    if n_a == 0:
        raise NotImplementedError("a trunk without mixer-A layers is not supported")

    h_b, h_kv = h_next
    wd = jnp.concatenate([
        w_dkv[:, :kv_rank],
        jnp.pad(w_dkv[:, kv_rank:], ((0, 0), (0, 128 - ROPE_DIM))),
        jnp.pad(_rotate_half_columns(w_dkv[:, kv_rank:]), ((0, 0), (0, 128 - ROPE_DIM))),
    ], axis=1).astype(BF16)
    k_cat, v_t = _kv_side(h_kv.reshape(b, s, d), wd, kv_latent_norm, w_uk.astype(BF16),
                          w_uv.T.astype(BF16), cos, sin, heads)

    for j in range(n_b):
        layer = n_a + j
        wq = w_uq[j].reshape(q_rank, heads, QK_DIM)
        w_n_t = wq[:, :, :HEAD_DIM].reshape(q_rank, heads * HEAD_DIM).T.astype(BF16)
        w_r_t = wq[:, :, HEAD_DIM:].reshape(q_rank, heads * ROPE_DIM).T.astype(BF16)
        w_rot_t = _rotate_half_columns(wq[:, :, HEAD_DIM:]).reshape(
            q_rank, heads * ROPE_DIM).T.astype(BF16)
        q_t, sg = _q_side(h_b.reshape(b, s, d), w_in_b[j].astype(BF16), q_latent_norm[j],
                          w_n_t, w_r_t, w_rot_t, cos.T, sin.T, heads)
        og = _mla_attention(q_t, k_cat, v_t, sg)
        x2d, h_next = finish_layer(layer, og, x2d, w_out_b[j])
        if h_next:
            h_b = h_next[0]
    return x2d.reshape(b, s, d)
```

```python
import functools
import math

import jax
import jax.numpy as jnp
from jax import lax
from jax.experimental import pallas as pl
from jax.experimental.pallas import tpu as pltpu

RMS_EPS = 1e-6
NEG_INF = -1e30
HEAD_DIM = 128
ROPE_DIM = 64
QK_DIM = HEAD_DIM + ROPE_DIM
CHUNK = 64
HEADS_PER_STEP = 2
ROPE_THETA = 10000.0
LOG2_E = math.log2(math.e)
EXP2_UNDERFLOW = -160.0

V7X_VMEM_LIMIT_BYTES = 56 * 1024 * 1024
BF16 = jnp.bfloat16
F32 = jnp.float32


def _tile(n, pref):
    t = min(n, pref)
    while n % t:
        t //= 2
    return t


def _params(*sem):
    return pltpu.CompilerParams(dimension_semantics=sem,
                                vmem_limit_bytes=V7X_VMEM_LIMIT_BYTES)


def _resident(shape):
    nd = len(shape)
    return pl.BlockSpec(shape, lambda *_: (0,) * nd, pipeline_mode=pl.Buffered(1))


def _rms_scale(v):
    return lax.rsqrt(jnp.mean(v * v, axis=-1, keepdims=True) + RMS_EPS)


def _sigmoid(v):
    return 1.0 / (1.0 + jnp.exp(-v))


def _dot(a, b):
    return jnp.dot(a, b, preferred_element_type=F32)


def _dot_nt(a, b):
    return lax.dot_general(a, b, (((1,), (1,)), ((), ())), preferred_element_type=F32)


def _proj_a_kernel(x_ref, g_ref, w_ref, o_ref, h_scr, *, q_blocks, q_scale):
    j = pl.program_id(1)

    @pl.when(j == 0)
    def _():
        x = x_ref[...]
        h_scr[...] = (x * _rms_scale(x) * g_ref[...]).astype(BF16)

    acc = _dot(h_scr[...], w_ref[...].astype(BF16))

    @pl.when(j < q_blocks)
    def _():
        o_ref[...] = (acc * q_scale).astype(BF16)

    @pl.when((j >= q_blocks) & (j < 3 * q_blocks))
    def _():
        o_ref[...] = acc.astype(BF16)

    @pl.when(j >= 3 * q_blocks)
    def _():
        o_ref[...] = (acc * _sigmoid(acc)).astype(BF16)


def _proj_a(x2d, gain, w_in):
    t, d = x2d.shape
    n = w_in.shape[1]
    width = n // 4
    tm = _tile(t, 1024)
    tn = _tile(width, 1024)
    return pl.pallas_call(
        functools.partial(_proj_a_kernel, q_blocks=width // tn,
                          q_scale=HEAD_DIM ** -0.5 * LOG2_E),
        out_shape=jax.ShapeDtypeStruct((t, n), BF16),
        grid=(t // tm, n // tn),
        in_specs=[
            pl.BlockSpec((tm, d), lambda i, j: (i, 0)),
            pl.BlockSpec((1, d), lambda i, j: (0, 0)),
            pl.BlockSpec((d, tn), lambda i, j: (0, j)),
        ],
        out_specs=pl.BlockSpec((tm, tn), lambda i, j: (i, j)),
        scratch_shapes=[pltpu.VMEM((tm, d), BF16)],
        compiler_params=_params("parallel", "arbitrary"),
        name="proj_a",
    )(x2d, gain.reshape(1, d), w_in)


def _sb_attn_kernel(q_ref, k_ref, v_ref, sg_ref, upper_ref, o_ref, vt_scr, acc_scr, carry_scr,
                    *, tq, tk):
    i = pl.program_id(2)
    heads = []
    for hh in range(HEADS_PER_STEP):
        cols = pl.ds(hh * HEAD_DIM, HEAD_DIM)
        heads.append(functools.partial(
            _sb_head, q_ref.at[:, cols], k_ref.at[:, cols], v_ref.at[:, cols], sg_ref.at[:, cols],
            upper_ref, o_ref.at[:, cols], vt_scr.at[hh], acc_scr.at[hh], carry_scr.at[hh],
            i=i, tq=tq, tk=tk))

    @pl.when(i == 0)
    def _():
        for head in heads:
            head(first_step=True)

    @pl.when(i > 0)
    def _():
        for head in heads:
            head(first_step=False)


def _sb_head(q_ref, k_ref, v_ref, sg_ref, upper_ref, o_ref, vt_scr, acc_scr, carry_scr,
             *, i, tq, tk, first_step):
    if first_step:
        for c in range(vt_scr.shape[0]):
            vt_scr[c] = v_ref[c * tk:(c + 1) * tk, :].T

    q_t = q_ref[...].T
    upper = upper_ref[...]

    def scores(blk, half):
        start = pl.multiple_of(blk * tk, tk)
        return _dot(k_ref[pl.ds(start, tk), :], q_t[:, half * tk:(half + 1) * tk])

    def stage(z, strict):
        soft = jnp.log2(1.0 + jnp.exp2(-jnp.abs(z)))
        log_beta = jnp.minimum(z, 0.0) - soft
        log_keep = log_beta - z
        if strict is not None:
            log_keep = jnp.where(strict, log_keep, 0.0)
        later = _dot(upper, log_keep.astype(BF16))
        return log_beta + later, jnp.sum(log_keep, axis=0, keepdims=True)

    def group(chains, carry, acc):
        raw = [scores(blk, half) for blk, _, half, _ in chains]
        staged = [stage(z, strict) for z, (_, strict, _, _) in zip(raw, chains)]
        carry, acc = list(carry), list(acc)
        for (blk, strict, half, valid), (pre, total) in zip(chains, staged):
            w = jnp.exp2(pre + carry[half])
            if strict is not None:
                w = jnp.where(strict, w, 0.0)
            if valid is not None:
                w = jnp.where(valid, w, 0.0)
                total = jnp.where(valid, total, 0.0)
            acc[half] = acc[half] + _dot(vt_scr[blk], w.astype(BF16))
            carry[half] = carry[half] + total
        acc_scr[...] = jnp.concatenate(acc, axis=1)
        carry_scr[...] = jnp.concatenate(carry, axis=1)

    key = lax.broadcasted_iota(jnp.int32, (tk, tk), 0)
    qry = lax.broadcasted_iota(jnp.int32, (tk, tk), 1)
    strict = key < qry
    first = 2 * i
    zeros = ([jnp.zeros((1, tk), F32)] * 2, [jnp.zeros((HEAD_DIM, tk), F32)] * 2)

    if first_step:
        group([(1, strict, 1, None), (0, strict, 0, None), (0, None, 1, None)], *zeros)
    else:
        group([(first + 1, strict, 1, None), (first, strict, 0, None),
               (first, None, 1, None), (first - 1, None, 0, None)], *zeros)

    def live(blk):
        return (blk >= 0) & (jnp.max(carry_scr[...]) > EXP2_UNDERFLOW)

    def body(blk):
        carry, acc = carry_scr[...], acc_scr[...]
        group([(blk, None, 1, None), (jnp.maximum(blk - 1, 0), None, 0, blk >= 1)],
              [carry[:, :tk], carry[:, tk:]], [acc[:, :tk], acc[:, tk:]])
        return blk - 1

    lax.while_loop(live, body, first - 1)
    o_ref[...] = (acc_scr[...].T * sg_ref[...].astype(F32)).astype(BF16)


def _sb_attention(qkvg, heads):
    b, s, _ = qkvg.shape
    tq = _tile(s, 512)
    tk = _tile(s, 256)
    assert tq == 2 * tk, "the kernel walks the query block as two halves of tk queries"
    groups, width = heads // HEADS_PER_STEP, HEADS_PER_STEP * HEAD_DIM
    upper = (lax.broadcasted_iota(jnp.int32, (tk, tk), 1)
             > lax.broadcasted_iota(jnp.int32, (tk, tk), 0)).astype(BF16)
    return pl.pallas_call(
        functools.partial(_sb_attn_kernel, tq=tq, tk=tk),
        out_shape=jax.ShapeDtypeStruct((b, s, heads * HEAD_DIM), BF16),
        grid=(b, groups, s // tq),
        in_specs=[
            pl.BlockSpec((None, tq, width), lambda bi, h, i: (bi, i, h)),
            pl.BlockSpec((None, s, width), lambda bi, h, i: (bi, 0, groups + h)),
            pl.BlockSpec((None, s, width), lambda bi, h, i: (bi, 0, 2 * groups + h)),
            pl.BlockSpec((None, tq, width), lambda bi, h, i: (bi, i, 3 * groups + h)),
            pl.BlockSpec((tk, tk), lambda bi, h, i: (0, 0)),
        ],
        out_specs=pl.BlockSpec((None, tq, width), lambda bi, h, i: (bi, i, h)),
        scratch_shapes=[pltpu.VMEM((HEADS_PER_STEP, s // tk, HEAD_DIM, tk), BF16),
                        pltpu.VMEM((HEADS_PER_STEP, HEAD_DIM, tq), F32),
                        pltpu.VMEM((HEADS_PER_STEP, 1, tq), F32)],
        compiler_params=_params("parallel", "parallel", "arbitrary"),
        name="sb_attn",
    )(qkvg, qkvg, qkvg, qkvg, upper)


def _out_kernel(og_ref, x_ref, p_ref, wout_ref, gpost_ref, wpp_ref, wpg_ref, gnext_ref,
                xo_ref, *next_refs):
    y = _dot(og_ref[...], wout_ref[...])
    x1 = x_ref[...] + y * _rms_scale(y) * gpost_ref[...]
    gate = _dot(x1.astype(BF16), wpg_ref[...])
    emb = _dot(p_ref[...].astype(BF16), wpp_ref[...])
    x2 = x1 + emb * _sigmoid(gate)
    xo_ref[...] = x2
    if next_refs:
        normed = x2 * _rms_scale(x2)
        for n, ref in enumerate(next_refs):
            ref[...] = (normed * gnext_ref[n:n + 1, :]).astype(BF16)


def _out_proj(og2d, x2d, p2d, w_out, g_post, w_pp, w_pg, next_gains):
    t, d = x2d.shape
    width = og2d.shape[1]
    ple = p2d.shape[1]
    tm = _tile(t, 512)
    n_next = len(next_gains)
    gnext = jnp.stack(next_gains) if n_next else jnp.zeros((1, d), F32)
    row = lambda i: (i, 0)
    outs = pl.pallas_call(
        _out_kernel,
        out_shape=[jax.ShapeDtypeStruct((t, d), F32)]
        + [jax.ShapeDtypeStruct((t, d), BF16)] * n_next,
        grid=(t // tm,),
        in_specs=[
            pl.BlockSpec((tm, width), row),
            pl.BlockSpec((tm, d), row),
            pl.BlockSpec((tm, ple), row),
            _resident((width, d)),
            _resident((1, d)),
            _resident((ple, d)),
            _resident((d, d)),
            _resident(gnext.shape),
        ],
        out_specs=[pl.BlockSpec((tm, d), row)] * (1 + n_next),
        compiler_params=_params("parallel"),
        name="out_proj",
    )(og2d, x2d, p2d, w_out, g_post.reshape(1, d), w_pp, w_pg, gnext)
    return outs[0], list(outs[1:])


def _kv_kernel(h_ref, wd_ref, gl_ref, wuk_ref, wuvt_ref, cos_ref, sin_ref, k_ref, vt_ref,
               *, rank, heads):
    c = _dot(h_ref[...], wd_ref[...])
    ckv = c[:, :rank]
    k_rope = c[:, rank:rank + ROPE_DIM]
    k_rot = c[:, rank + 128:rank + 128 + ROPE_DIM]
    latent = (ckv * _rms_scale(ckv) * gl_ref[...]).astype(BF16)
    k_nope = _dot(latent, wuk_ref[...]).astype(BF16)
    vt_ref[...] = _dot_nt(wuvt_ref[...], latent).astype(BF16)
    roped = (k_rope * cos_ref[...] + k_rot * sin_ref[...]).astype(BF16)
    for h in range(heads):
        k_ref[h, :, 0:HEAD_DIM] = k_nope[:, h * HEAD_DIM:(h + 1) * HEAD_DIM]
        k_ref[h, :, HEAD_DIM:QK_DIM] = roped


def _kv_side(hkv, wd, g_latent, w_uk, w_uv_t, cos, sin, heads):
    b, s, d = hkv.shape
    rank = w_uk.shape[0]
    tm = _tile(s, 512)
    row = lambda bi, i: (bi, i, 0)
    return pl.pallas_call(
        functools.partial(_kv_kernel, rank=rank, heads=heads),
        out_shape=[jax.ShapeDtypeStruct((b, heads, s, QK_DIM), BF16),
                   jax.ShapeDtypeStruct((b, heads * HEAD_DIM, s), BF16)],
        grid=(b, s // tm),
        in_specs=[
            pl.BlockSpec((None, tm, d), row),
            _resident(wd.shape),
            _resident((1, rank)),
            _resident(w_uk.shape),
            _resident(w_uv_t.shape),
            pl.BlockSpec((tm, ROPE_DIM), lambda bi, i: (i, 0)),
            pl.BlockSpec((tm, ROPE_DIM), lambda bi, i: (i, 0)),
        ],
        out_specs=[pl.BlockSpec((None, heads, tm, QK_DIM), lambda bi, i: (bi, 0, i, 0)),
                   pl.BlockSpec((None, heads * HEAD_DIM, tm), lambda bi, i: (bi, 0, i))],
        compiler_params=_params("parallel", "parallel"),
        name="kv_side",
    )(hkv, wd, g_latent.reshape(1, rank), w_uk, w_uv_t, cos, sin)


def _q_kernel(h_ref, win_ref, gq_ref, wnt_ref, wrt_ref, wrott_ref, cost_ref, sint_ref,
              qt_ref, sg_ref, *, rank, heads, scale):
    c = _dot(h_ref[...], win_ref[...])
    cq = c[:, :rank]
    gate = c[:, rank:]
    sg_ref[...] = (gate * _sigmoid(gate)).astype(BF16)
    latent = (cq * _rms_scale(cq) * gq_ref[...]).astype(BF16)
    nope_t = (_dot_nt(wnt_ref[...], latent) * scale).astype(BF16)
    cos_t = jnp.tile(cost_ref[...], (heads, 1))
    sin_t = jnp.tile(sint_ref[...], (heads, 1))
    roped_t = ((_dot_nt(wrt_ref[...], latent) * cos_t + _dot_nt(wrott_ref[...], latent) * sin_t)
               * scale).astype(BF16)
    for h in range(heads):
        qt_ref[h, 0:HEAD_DIM, :] = nope_t[h * HEAD_DIM:(h + 1) * HEAD_DIM, :]
        qt_ref[h, HEAD_DIM:QK_DIM, :] = roped_t[h * ROPE_DIM:(h + 1) * ROPE_DIM, :]


def _q_side(h, w_in, g_latent, w_n_t, w_r_t, w_rot_t, cos_t, sin_t, heads):
    b, s, d = h.shape
    rank = w_n_t.shape[1]
    width = w_in.shape[1] - rank
    tm = _tile(s, 512)
    row = lambda bi, i: (bi, i, 0)
    return pl.pallas_call(
        functools.partial(_q_kernel, rank=rank, heads=heads, scale=QK_DIM ** -0.5 * LOG2_E),
        out_shape=[jax.ShapeDtypeStruct((b, heads, QK_DIM, s), BF16),
                   jax.ShapeDtypeStruct((b, s, width), BF16)],
        grid=(b, s // tm),
        in_specs=[
            pl.BlockSpec((None, tm, d), row),
            _resident(w_in.shape),
            _resident((1, rank)),
            _resident(w_n_t.shape),
            _resident(w_r_t.shape),
            _resident(w_rot_t.shape),
            pl.BlockSpec((ROPE_DIM, tm), lambda bi, i: (0, i)),
            pl.BlockSpec((ROPE_DIM, tm), lambda bi, i: (0, i)),
        ],
        out_specs=[pl.BlockSpec((None, heads, QK_DIM, tm), lambda bi, i: (bi, 0, 0, i)),
                   pl.BlockSpec((None, tm, width), row)],
        compiler_params=_params("parallel", "parallel"),
        name="q_side",
    )(h, w_in, g_latent.reshape(1, rank), w_n_t, w_r_t, w_rot_t, cos_t, sin_t)


def _mla_attn_kernel(qt_ref, k_ref, vt_ref, sg_ref, o_ref, s_scr, p_scr, acc_scr, *, tq, tk):
    i = pl.program_id(2)
    heads = []
    for hh in range(HEADS_PER_STEP):
        cols = pl.ds(hh * HEAD_DIM, HEAD_DIM)
        heads.append(functools.partial(
            _mla_head, qt_ref.at[hh], k_ref.at[hh], vt_ref.at[cols, :], sg_ref.at[:, cols],
            o_ref.at[:, cols], s_scr.at[hh], p_scr.at[hh], acc_scr.at[hh], i=i, tq=tq, tk=tk))

    @pl.when(i == 0)
    def _():
        for head in heads:
            head(first_step=True)

    @pl.when(i > 0)
    def _():
        for head in heads:
            head(first_step=False)


def _mla_head(qt_ref, k_ref, vt_ref, sg_ref, o_ref, s_scr, p_scr, acc_scr,
              *, i, tq, tk, first_step):
    nblk = (i + 1) * (tq // tk)
    q_t = qt_ref[...]
    acc_scr[...] = jnp.zeros_like(acc_scr)

    def key_start(n):
        return pl.multiple_of(n * tk, tk)

    def step(n, parity, stats, run, *, merge, softmax, score, mask=None, steady=False):
        if merge and steady:
            out = _dot(vt_ref[:, pl.ds(key_start(n - 2), tk)], p_scr[parity])
        if score:
            s_scr[parity] = _dot(k_ref[pl.ds(key_start(n), tk), :], q_t)
        if merge and not steady:
            out = _dot(vt_ref[:, pl.ds(key_start(n - 2), tk)], p_scr[parity])
        m_c, l_c = stats
        if softmax:
            s = s_scr[1 - parity]
            if mask is not None:
                s = jnp.where(mask, s, NEG_INF)
            m_next = jnp.max(s, axis=0, keepdims=True)
            prob = jnp.exp2(s - m_next)
            p_scr[1 - parity] = prob.astype(BF16)
            stats = (m_next, jnp.sum(prob, axis=0, keepdims=True))
        if merge:
            m_run, l_run = run
            m_new = jnp.maximum(m_run, m_c)
            alpha = jnp.exp2(m_run - m_new)
            weight = jnp.exp2(m_c - m_new)
            acc_scr[...] = acc_scr[...] * alpha + out * weight
            run = (m_new, l_run * alpha + l_c * weight)
        return stats, run

    def finish(run):
        out = (acc_scr[...] / run[1]).T
        o_ref[...] = (out * sg_ref[...].astype(F32)).astype(BF16)

    kchunk = lax.broadcasted_iota(jnp.int32, (tk, tq), 0) // CHUNK
    qchunk = lax.broadcasted_iota(jnp.int32, (tk, tq), 1) // CHUNK
    first_diag = kchunk <= qchunk
    second_diag = (kchunk + tk // CHUNK) <= qchunk
    row = jnp.zeros((1, tq), F32)
    state = ((row, row), (jnp.full((1, tq), NEG_INF, F32), row))
    everything = dict(merge=True, softmax=True, score=True)

    if first_step:
        st = step(0, 0, *state, merge=False, softmax=False, score=True)
        st = step(1, 1, *st, merge=False, softmax=True, score=True, mask=first_diag)
        st = step(2, 0, *st, merge=True, softmax=True, score=False, mask=second_diag)
        st = step(3, 1, *st, merge=True, softmax=False, score=False)
        finish(st[1])
    else:
        st = step(0, 0, *state, merge=False, softmax=False, score=True)
        st = step(1, 1, *st, merge=False, softmax=True, score=True)
        st = step(2, 0, *st, **everything)

        def pair(t, st):
            st = step(3 + 2 * t, 1, *st, steady=True, **everything)
            return step(4 + 2 * t, 0, *st, steady=True, **everything)

        st = lax.fori_loop(0, (nblk - 4) // 2, pair, st)
        st = step(nblk - 1, 1, *st, mask=first_diag, **everything)
        st = step(nblk, 0, *st, merge=True, softmax=True, score=False, mask=second_diag)
        st = step(nblk + 1, 1, *st, merge=True, softmax=False, score=False)
        finish(st[1])


def _mla_attention(q_t, k_cat, v_t, sg):
    b, heads, _, s = q_t.shape
    tq = _tile(s, 512)
    tk = _tile(s, 256)
    assert tq == 2 * tk, "the pipeline below peels exactly two diagonal key blocks"
    width = HEADS_PER_STEP * HEAD_DIM
    return pl.pallas_call(
        functools.partial(_mla_attn_kernel, tq=tq, tk=tk),
        out_shape=jax.ShapeDtypeStruct((b, s, heads * HEAD_DIM), BF16),
        grid=(b, heads // HEADS_PER_STEP, s // tq),
        in_specs=[
            pl.BlockSpec((None, HEADS_PER_STEP, QK_DIM, tq), lambda bi, h, i: (bi, h, 0, i)),
            pl.BlockSpec((None, HEADS_PER_STEP, s, QK_DIM), lambda bi, h, i: (bi, h, 0, 0)),
            pl.BlockSpec((None, width, s), lambda bi, h, i: (bi, h, 0)),
            pl.BlockSpec((None, tq, width), lambda bi, h, i: (bi, i, h)),
        ],
        out_specs=pl.BlockSpec((None, tq, width), lambda bi, h, i: (bi, i, h)),
        scratch_shapes=[pltpu.VMEM((HEADS_PER_STEP, 2, tk, tq), F32),
                        pltpu.VMEM((HEADS_PER_STEP, 2, tk, tq), BF16),
                        pltpu.VMEM((HEADS_PER_STEP, HEAD_DIM, tq), F32)],
        compiler_params=_params("parallel", "parallel", "arbitrary"),
        name="mla_attn",
    )(q_t, k_cat, v_t, sg)


def _rotate_half_columns(w):
    half = ROPE_DIM // 2
    return jnp.concatenate([-w[..., half:], w[..., :half]], axis=-1)


def _rope_tables(s):
    inv = 1.0 / (ROPE_THETA ** (jnp.arange(0, ROPE_DIM, 2, dtype=F32) / ROPE_DIM))
    ang = jnp.arange(s, dtype=F32)[:, None] * inv[None, :]
    cos, sin = jnp.cos(ang), jnp.sin(ang)
    return jnp.tile(cos, (1, 2)), jnp.tile(sin, (1, 2))


def kernel(x, p, norm_pre, norm_post, w_in_a, w_out_a, w_in_b, q_latent_norm, w_uq, w_out_b,
           kv_norm, w_dkv, kv_latent_norm, w_uk, w_uv, w_ple_proj, w_ple_gate):
    b, s, d = x.shape
    t = b * s
    n_a, n_b = w_in_a.shape[0], w_in_b.shape[0]
    heads = d // HEAD_DIM
    q_rank = w_uq.shape[1]
    kv_rank = w_uk.shape[0]
    cos, sin = _rope_tables(s)

    x2d = x.reshape(t, d)
    p2d = p.reshape(p.shape[0], t, p.shape[-1])

    def finish_layer(layer, og, x2d, w_out):
        if layer + 1 < n_a:
            gains = [norm_pre[layer + 1]]
        elif layer + 1 == n_a and n_b:
            gains = [norm_pre[layer + 1], kv_norm]
        elif layer + 1 < n_a + n_b:
            gains = [norm_pre[layer + 1]]
        else:
            gains = []
        return _out_proj(og.reshape(t, -1), x2d, p2d[layer], w_out.astype(BF16),
                         norm_post[layer], w_ple_proj[layer].astype(BF16),
                         w_ple_gate[layer].astype(BF16), gains)

    h_next = None
    for i in range(n_a):
        qkvg = _proj_a(x2d, norm_pre[i], w_in_a[i])
        og = _sb_attention(qkvg.reshape(b, s, -1), heads)
        x2d, h_next = finish_layer(i, og, x2d, w_out_a[i])

    if n_b == 0:
        return x2d.reshape(b, s, d)
    if n_a == 0:
        raise NotImplementedError("a trunk without mixer-A layers is not supported")

    h_b, h_kv = h_next
    wd = jnp.concatenate([
        w_dkv[:, :kv_rank],
        jnp.pad(w_dkv[:, kv_rank:], ((0, 0), (0, 128 - ROPE_DIM))),
        jnp.pad(_rotate_half_columns(w_dkv[:, kv_rank:]), ((0, 0), (0, 128 - ROPE_DIM))),
    ], axis=1).astype(BF16)
    k_cat, v_t = _kv_side(h_kv.reshape(b, s, d), wd, kv_latent_norm, w_uk.astype(BF16),
                          w_uv.T.astype(BF16), cos, sin, heads)

    for j in range(n_b):
        layer = n_a + j
        wq = w_uq[j].reshape(q_rank, heads, QK_DIM)
        w_n_t = wq[:, :, :HEAD_DIM].reshape(q_rank, heads * HEAD_DIM).T.astype(BF16)
        w_r_t = wq[:, :, HEAD_DIM:].reshape(q_rank, heads * ROPE_DIM).T.astype(BF16)
        w_rot_t = _rotate_half_columns(wq[:, :, HEAD_DIM:]).reshape(
            q_rank, heads * ROPE_DIM).T.astype(BF16)
        q_t, sg = _q_side(h_b.reshape(b, s, d), w_in_b[j].astype(BF16), q_latent_norm[j],
                          w_n_t, w_r_t, w_rot_t, cos.T, sin.T, heads)
        og = _mla_attention(q_t, k_cat, v_t, sg)
        x2d, h_next = finish_layer(layer, og, x2d, w_out_b[j])
        if h_next:
            h_b = h_next[0]
    return x2d.reshape(b, s, d)
```

```python
import functools
import math

import jax
import jax.numpy as jnp
from jax import lax
from jax.experimental import pallas as pl
from jax.experimental.pallas import tpu as pltpu

RMS_EPS = 1e-6
NEG_INF = -1e30
HEAD_DIM = 128
ROPE_DIM = 64
QK_DIM = HEAD_DIM + ROPE_DIM
CHUNK = 64
HEADS_PER_STEP = 2
SCORES_AHEAD = 2
ROPE_THETA = 10000.0
LOG2_E = math.log2(math.e)
EXP2_UNDERFLOW = -160.0

V7X_VMEM_LIMIT_BYTES = 56 * 1024 * 1024
BF16 = jnp.bfloat16
F32 = jnp.float32


def _tile(n, pref):
    t = min(n, pref)
    while n % t:
        t //= 2
    return t


def _params(*sem):
    return pltpu.CompilerParams(dimension_semantics=sem,
                                vmem_limit_bytes=V7X_VMEM_LIMIT_BYTES)


def _resident(shape):
    nd = len(shape)
    return pl.BlockSpec(shape, lambda *_: (0,) * nd, pipeline_mode=pl.Buffered(1))


def _rms_scale(v):
    return lax.rsqrt(jnp.mean(v * v, axis=-1, keepdims=True) + RMS_EPS)


def _sigmoid(v):
    return 1.0 / (1.0 + jnp.exp(-v))


def _dot(a, b):
    return jnp.dot(a, b, preferred_element_type=F32)


def _dot_nt(a, b):
    return lax.dot_general(a, b, (((1,), (1,)), ((), ())), preferred_element_type=F32)


def _proj_a_kernel(x_ref, g_ref, w_ref, o_ref, h_scr, *, q_blocks, q_scale):
    j = pl.program_id(1)

    @pl.when(j == 0)
    def _():
        x = x_ref[...]
        h_scr[...] = (x * _rms_scale(x) * g_ref[...]).astype(BF16)

    acc = _dot(h_scr[...], w_ref[...].astype(BF16))

    @pl.when(j < q_blocks)
    def _():
        o_ref[...] = (acc * q_scale).astype(BF16)

    @pl.when((j >= q_blocks) & (j < 3 * q_blocks))
    def _():
        o_ref[...] = acc.astype(BF16)

    @pl.when(j >= 3 * q_blocks)
    def _():
        o_ref[...] = (acc * _sigmoid(acc)).astype(BF16)


def _proj_a(x2d, gain, w_in):
    t, d = x2d.shape
    n = w_in.shape[1]
    width = n // 4
    tm = _tile(t, 1024)
    tn = _tile(width, 1024)
    return pl.pallas_call(
        functools.partial(_proj_a_kernel, q_blocks=width // tn,
                          q_scale=HEAD_DIM ** -0.5 * LOG2_E),
        out_shape=jax.ShapeDtypeStruct((t, n), BF16),
        grid=(t // tm, n // tn),
        in_specs=[
            pl.BlockSpec((tm, d), lambda i, j: (i, 0)),
            pl.BlockSpec((1, d), lambda i, j: (0, 0)),
            pl.BlockSpec((d, tn), lambda i, j: (0, j)),
        ],
        out_specs=pl.BlockSpec((tm, tn), lambda i, j: (i, j)),
        scratch_shapes=[pltpu.VMEM((tm, d), BF16)],
        compiler_params=_params("parallel", "arbitrary"),
        name="proj_a",
    )(x2d, gain.reshape(1, d), w_in)


def _sb_attn_kernel(q_ref, k_ref, v_ref, sg_ref, upper_ref, o_ref, vt_scr, acc_scr, carry_scr,
                    *, tq, tk):
    i = pl.program_id(2)
    heads = []
    for hh in range(HEADS_PER_STEP):
        cols = pl.ds(hh * HEAD_DIM, HEAD_DIM)
        heads.append(functools.partial(
            _sb_head, q_ref.at[:, cols], k_ref.at[:, cols], v_ref.at[:, cols], sg_ref.at[:, cols],
            upper_ref, o_ref.at[:, cols], vt_scr.at[hh], acc_scr.at[hh], carry_scr.at[hh],
            i=i, tq=tq, tk=tk))

    @pl.when(i == 0)
    def _():
        for head in heads:
            head(first_step=True)

    @pl.when(i > 0)
    def _():
        for head in heads:
            head(first_step=False)


def _sb_head(q_ref, k_ref, v_ref, sg_ref, upper_ref, o_ref, vt_scr, acc_scr, carry_scr,
             *, i, tq, tk, first_step):
    if first_step:
        for c in range(vt_scr.shape[0]):
            vt_scr[c] = v_ref[c * tk:(c + 1) * tk, :].T

    q_t = q_ref[...].T
    upper = upper_ref[...]

    def scores(blk, half):
        start = pl.multiple_of(blk * tk, tk)
        return _dot(k_ref[pl.ds(start, tk), :], q_t[:, half * tk:(half + 1) * tk])

    def stage(z, strict):
        soft = jnp.log2(1.0 + jnp.exp2(-jnp.abs(z)))
        log_beta = jnp.minimum(z, 0.0) - soft
        log_keep = log_beta - z
        if strict is not None:
            log_keep = jnp.where(strict, log_keep, 0.0)
        later = _dot(upper, log_keep.astype(BF16))
        return log_beta + later, jnp.sum(log_keep, axis=0, keepdims=True)

    def group(chains, carry, acc):
        raw = [scores(blk, half) for blk, _, half, _ in chains[:SCORES_AHEAD]]
        staged = []
        for n, (_, strict, _, _) in enumerate(chains):
            if n + SCORES_AHEAD < len(chains):
                blk, _, half, _ = chains[n + SCORES_AHEAD]
                raw.append(scores(blk, half))
            staged.append(stage(raw[n], strict))
        carry, acc = list(carry), list(acc)
        for (blk, strict, half, valid), (pre, total) in zip(chains, staged):
            w = jnp.exp2(pre + carry[half])
            if strict is not None:
                w = jnp.where(strict, w, 0.0)
            if valid is not None:
                w = jnp.where(valid, w, 0.0)
                total = jnp.where(valid, total, 0.0)
            acc[half] = acc[half] + _dot(vt_scr[blk], w.astype(BF16))
            carry[half] = carry[half] + total
        acc_scr[...] = jnp.concatenate(acc, axis=1)
        carry_scr[...] = jnp.concatenate(carry, axis=1)

    key = lax.broadcasted_iota(jnp.int32, (tk, tk), 0)
    qry = lax.broadcasted_iota(jnp.int32, (tk, tk), 1)
    strict = key < qry
    first = 2 * i
    zeros = ([jnp.zeros((1, tk), F32)] * 2, [jnp.zeros((HEAD_DIM, tk), F32)] * 2)

    if first_step:
        group([(1, strict, 1, None), (0, strict, 0, None), (0, None, 1, None)], *zeros)
    else:
        group([(first + 1, strict, 1, None), (first, strict, 0, None),
               (first, None, 1, None), (first - 1, None, 0, None)], *zeros)

    def live(blk):
        return (blk >= 0) & (jnp.max(carry_scr[...]) > EXP2_UNDERFLOW)

    def body(blk):
        carry, acc = carry_scr[...], acc_scr[...]
        group([(blk, None, 1, None), (jnp.maximum(blk - 1, 0), None, 0, blk >= 1)],
              [carry[:, :tk], carry[:, tk:]], [acc[:, :tk], acc[:, tk:]])
        return blk - 1

    lax.while_loop(live, body, first - 1)
    o_ref[...] = (acc_scr[...].T * sg_ref[...].astype(F32)).astype(BF16)


def _sb_attention(qkvg, heads):
    b, s, _ = qkvg.shape
    tq = _tile(s, 512)
    tk = _tile(s, 256)
    assert tq == 2 * tk, "the kernel walks the query block as two halves of tk queries"
    groups, width = heads // HEADS_PER_STEP, HEADS_PER_STEP * HEAD_DIM
    upper = (lax.broadcasted_iota(jnp.int32, (tk, tk), 1)
             > lax.broadcasted_iota(jnp.int32, (tk, tk), 0)).astype(BF16)
    return pl.pallas_call(
        functools.partial(_sb_attn_kernel, tq=tq, tk=tk),
        out_shape=jax.ShapeDtypeStruct((b, s, heads * HEAD_DIM), BF16),
        grid=(b, groups, s // tq),
        in_specs=[
            pl.BlockSpec((None, tq, width), lambda bi, h, i: (bi, i, h)),
            pl.BlockSpec((None, s, width), lambda bi, h, i: (bi, 0, groups + h)),
            pl.BlockSpec((None, s, width), lambda bi, h, i: (bi, 0, 2 * groups + h)),
            pl.BlockSpec((None, tq, width), lambda bi, h, i: (bi, i, 3 * groups + h)),
            pl.BlockSpec((tk, tk), lambda bi, h, i: (0, 0)),
        ],
        out_specs=pl.BlockSpec((None, tq, width), lambda bi, h, i: (bi, i, h)),
        scratch_shapes=[pltpu.VMEM((HEADS_PER_STEP, s // tk, HEAD_DIM, tk), BF16),
                        pltpu.VMEM((HEADS_PER_STEP, HEAD_DIM, tq), F32),
                        pltpu.VMEM((HEADS_PER_STEP, 1, tq), F32)],
        compiler_params=_params("parallel", "parallel", "arbitrary"),
        name="sb_attn",
    )(qkvg, qkvg, qkvg, qkvg, upper)


def _out_kernel(og_ref, x_ref, p_ref, wout_ref, gpost_ref, wpp_ref, wpg_ref, gnext_ref,
                xo_ref, *next_refs):
    y = _dot(og_ref[...], wout_ref[...])
    x1 = x_ref[...] + y * _rms_scale(y) * gpost_ref[...]
    gate = _dot(x1.astype(BF16), wpg_ref[...])
    emb = _dot(p_ref[...].astype(BF16), wpp_ref[...])
    x2 = x1 + emb * _sigmoid(gate)
    xo_ref[...] = x2
    if next_refs:
        normed = x2 * _rms_scale(x2)
        for n, ref in enumerate(next_refs):
            ref[...] = (normed * gnext_ref[n:n + 1, :]).astype(BF16)


def _out_proj(og2d, x2d, p2d, w_out, g_post, w_pp, w_pg, next_gains):
    t, d = x2d.shape
    width = og2d.shape[1]
    ple = p2d.shape[1]
    tm = _tile(t, 512)
    n_next = len(next_gains)
    gnext = jnp.stack(next_gains) if n_next else jnp.zeros((1, d), F32)
    row = lambda i: (i, 0)
    outs = pl.pallas_call(
        _out_kernel,
        out_shape=[jax.ShapeDtypeStruct((t, d), F32)]
        + [jax.ShapeDtypeStruct((t, d), BF16)] * n_next,
        grid=(t // tm,),
        in_specs=[
            pl.BlockSpec((tm, width), row),
            pl.BlockSpec((tm, d), row),
            pl.BlockSpec((tm, ple), row),
            _resident((width, d)),
            _resident((1, d)),
            _resident((ple, d)),
            _resident((d, d)),
            _resident(gnext.shape),
        ],
        out_specs=[pl.BlockSpec((tm, d), row)] * (1 + n_next),
        compiler_params=_params("parallel"),
        name="out_proj",
    )(og2d, x2d, p2d, w_out, g_post.reshape(1, d), w_pp, w_pg, gnext)
    return outs[0], list(outs[1:])


def _kv_kernel(h_ref, wd_ref, gl_ref, wuk_ref, wuvt_ref, cos_ref, sin_ref, k_ref, vt_ref,
               *, rank, heads):
    c = _dot(h_ref[...], wd_ref[...])
    ckv = c[:, :rank]
    k_rope = c[:, rank:rank + ROPE_DIM]
    k_rot = c[:, rank + 128:rank + 128 + ROPE_DIM]
    latent = (ckv * _rms_scale(ckv) * gl_ref[...]).astype(BF16)
    k_nope = _dot(latent, wuk_ref[...]).astype(BF16)
    vt_ref[...] = _dot_nt(wuvt_ref[...], latent).astype(BF16)
    roped = (k_rope * cos_ref[...] + k_rot * sin_ref[...]).astype(BF16)
    for h in range(heads):
        k_ref[h, :, 0:HEAD_DIM] = k_nope[:, h * HEAD_DIM:(h + 1) * HEAD_DIM]
        k_ref[h, :, HEAD_DIM:QK_DIM] = roped


def _kv_side(hkv, wd, g_latent, w_uk, w_uv_t, cos, sin, heads):
    b, s, d = hkv.shape
    rank = w_uk.shape[0]
    tm = _tile(s, 512)
    row = lambda bi, i: (bi, i, 0)
    return pl.pallas_call(
        functools.partial(_kv_kernel, rank=rank, heads=heads),
        out_shape=[jax.ShapeDtypeStruct((b, heads, s, QK_DIM), BF16),
                   jax.ShapeDtypeStruct((b, heads * HEAD_DIM, s), BF16)],
        grid=(b, s // tm),
        in_specs=[
            pl.BlockSpec((None, tm, d), row),
            _resident(wd.shape),
            _resident((1, rank)),
            _resident(w_uk.shape),
            _resident(w_uv_t.shape),
            pl.BlockSpec((tm, ROPE_DIM), lambda bi, i: (i, 0)),
            pl.BlockSpec((tm, ROPE_DIM), lambda bi, i: (i, 0)),
        ],
        out_specs=[pl.BlockSpec((None, heads, tm, QK_DIM), lambda bi, i: (bi, 0, i, 0)),
                   pl.BlockSpec((None, heads * HEAD_DIM, tm), lambda bi, i: (bi, 0, i))],
        compiler_params=_params("parallel", "parallel"),
        name="kv_side",
    )(hkv, wd, g_latent.reshape(1, rank), w_uk, w_uv_t, cos, sin)


def _q_kernel(h_ref, win_ref, gq_ref, wnt_ref, wrt_ref, wrott_ref, cost_ref, sint_ref,
              qt_ref, sg_ref, *, rank, heads, scale):
    c = _dot(h_ref[...], win_ref[...])
    cq = c[:, :rank]
    gate = c[:, rank:]
    sg_ref[...] = (gate * _sigmoid(gate)).astype(BF16)
    latent = (cq * _rms_scale(cq) * gq_ref[...]).astype(BF16)
    nope_t = (_dot_nt(wnt_ref[...], latent) * scale).astype(BF16)
    cos_t = jnp.tile(cost_ref[...], (heads, 1))
    sin_t = jnp.tile(sint_ref[...], (heads, 1))
    roped_t = ((_dot_nt(wrt_ref[...], latent) * cos_t + _dot_nt(wrott_ref[...], latent) * sin_t)
               * scale).astype(BF16)
    for h in range(heads):
        qt_ref[h, 0:HEAD_DIM, :] = nope_t[h * HEAD_DIM:(h + 1) * HEAD_DIM, :]
        qt_ref[h, HEAD_DIM:QK_DIM, :] = roped_t[h * ROPE_DIM:(h + 1) * ROPE_DIM, :]


def _q_side(h, w_in, g_latent, w_n_t, w_r_t, w_rot_t, cos_t, sin_t, heads):
    b, s, d = h.shape
    rank = w_n_t.shape[1]
    width = w_in.shape[1] - rank
    tm = _tile(s, 512)
    row = lambda bi, i: (bi, i, 0)
    return pl.pallas_call(
        functools.partial(_q_kernel, rank=rank, heads=heads, scale=QK_DIM ** -0.5 * LOG2_E),
        out_shape=[jax.ShapeDtypeStruct((b, heads, QK_DIM, s), BF16),
                   jax.ShapeDtypeStruct((b, s, width), BF16)],
        grid=(b, s // tm),
        in_specs=[
            pl.BlockSpec((None, tm, d), row),
            _resident(w_in.shape),
            _resident((1, rank)),
            _resident(w_n_t.shape),
            _resident(w_r_t.shape),
            _resident(w_rot_t.shape),
            pl.BlockSpec((ROPE_DIM, tm), lambda bi, i: (0, i)),
            pl.BlockSpec((ROPE_DIM, tm), lambda bi, i: (0, i)),
        ],
        out_specs=[pl.BlockSpec((None, heads, QK_DIM, tm), lambda bi, i: (bi, 0, 0, i)),
                   pl.BlockSpec((None, tm, width), row)],
        compiler_params=_params("parallel", "parallel"),
        name="q_side",
    )(h, w_in, g_latent.reshape(1, rank), w_n_t, w_r_t, w_rot_t, cos_t, sin_t)


def _mla_attn_kernel(qt_ref, k_ref, vt_ref, sg_ref, o_ref, s_scr, p_scr, acc_scr, *, tq, tk):
    i = pl.program_id(2)
    heads = []
    for hh in range(HEADS_PER_STEP):
        cols = pl.ds(hh * HEAD_DIM, HEAD_DIM)
        heads.append(functools.partial(
            _mla_head, qt_ref.at[hh], k_ref.at[hh], vt_ref.at[cols, :], sg_ref.at[:, cols],
            o_ref.at[:, cols], s_scr.at[hh], p_scr.at[hh], acc_scr.at[hh], i=i, tq=tq, tk=tk))

    @pl.when(i == 0)
    def _():
        for head in heads:
            head(first_step=True)

    @pl.when(i > 0)
    def _():
        for head in heads:
            head(first_step=False)


def _mla_head(qt_ref, k_ref, vt_ref, sg_ref, o_ref, s_scr, p_scr, acc_scr,
              *, i, tq, tk, first_step):
    nblk = (i + 1) * (tq // tk)
    q_t = qt_ref[...]
    acc_scr[...] = jnp.zeros_like(acc_scr)

    def key_start(n):
        return pl.multiple_of(n * tk, tk)

    def step(n, parity, stats, run, *, merge, softmax, score, mask=None, steady=False):
        if merge and steady:
            out = _dot(vt_ref[:, pl.ds(key_start(n - 2), tk)], p_scr[parity])
        if score:
            s_scr[parity] = _dot(k_ref[pl.ds(key_start(n), tk), :], q_t)
        if merge and not steady:
            out = _dot(vt_ref[:, pl.ds(key_start(n - 2), tk)], p_scr[parity])
        m_c, l_c = stats
        if softmax:
            s = s_scr[1 - parity]
            if mask is not None:
                s = jnp.where(mask, s, NEG_INF)
            m_next = jnp.max(s, axis=0, keepdims=True)
            prob = jnp.exp2(s - m_next)
            p_scr[1 - parity] = prob.astype(BF16)
            stats = (m_next, jnp.sum(prob, axis=0, keepdims=True))
        if merge:
            m_run, l_run = run
            m_new = jnp.maximum(m_run, m_c)
            alpha = jnp.exp2(m_run - m_new)
            weight = jnp.exp2(m_c - m_new)
            acc_scr[...] = acc_scr[...] * alpha + out * weight
            run = (m_new, l_run * alpha + l_c * weight)
        return stats, run

    def finish(run):
        out = (acc_scr[...] / run[1]).T
        o_ref[...] = (out * sg_ref[...].astype(F32)).astype(BF16)

    kchunk = lax.broadcasted_iota(jnp.int32, (tk, tq), 0) // CHUNK
    qchunk = lax.broadcasted_iota(jnp.int32, (tk, tq), 1) // CHUNK
    first_diag = kchunk <= qchunk
    second_diag = (kchunk + tk // CHUNK) <= qchunk
    row = jnp.zeros((1, tq), F32)
    state = ((row, row), (jnp.full((1, tq), NEG_INF, F32), row))
    everything = dict(merge=True, softmax=True, score=True)

    if first_step:
        st = step(0, 0, *state, merge=False, softmax=False, score=True)
        st = step(1, 1, *st, merge=False, softmax=True, score=True, mask=first_diag)
        st = step(2, 0, *st, merge=True, softmax=True, score=False, mask=second_diag)
        st = step(3, 1, *st, merge=True, softmax=False, score=False)
        finish(st[1])
    else:
        st = step(0, 0, *state, merge=False, softmax=False, score=True)
        st = step(1, 1, *st, merge=False, softmax=True, score=True)
        st = step(2, 0, *st, **everything)

        def pair(t, st):
            st = step(3 + 2 * t, 1, *st, steady=True, **everything)
            return step(4 + 2 * t, 0, *st, steady=True, **everything)

        st = lax.fori_loop(0, (nblk - 4) // 2, pair, st)
        st = step(nblk - 1, 1, *st, mask=first_diag, **everything)
        st = step(nblk, 0, *st, merge=True, softmax=True, score=False, mask=second_diag)
        st = step(nblk + 1, 1, *st, merge=True, softmax=False, score=False)
        finish(st[1])


def _mla_attention(q_t, k_cat, v_t, sg):
    b, heads, _, s = q_t.shape
    tq = _tile(s, 512)
    tk = _tile(s, 256)
    assert tq == 2 * tk, "the pipeline below peels exactly two diagonal key blocks"
    width = HEADS_PER_STEP * HEAD_DIM
    return pl.pallas_call(
        functools.partial(_mla_attn_kernel, tq=tq, tk=tk),
        out_shape=jax.ShapeDtypeStruct((b, s, heads * HEAD_DIM), BF16),
        grid=(b, heads // HEADS_PER_STEP, s // tq),
        in_specs=[
            pl.BlockSpec((None, HEADS_PER_STEP, QK_DIM, tq), lambda bi, h, i: (bi, h, 0, i)),
            pl.BlockSpec((None, HEADS_PER_STEP, s, QK_DIM), lambda bi, h, i: (bi, h, 0, 0)),
            pl.BlockSpec((None, width, s), lambda bi, h, i: (bi, h, 0)),
            pl.BlockSpec((None, tq, width), lambda bi, h, i: (bi, i, h)),
        ],
        out_specs=pl.BlockSpec((None, tq, width), lambda bi, h, i: (bi, i, h)),
        scratch_shapes=[pltpu.VMEM((HEADS_PER_STEP, 2, tk, tq), F32),
                        pltpu.VMEM((HEADS_PER_STEP, 2, tk, tq), BF16),
                        pltpu.VMEM((HEADS_PER_STEP, HEAD_DIM, tq), F32)],
        compiler_params=_params("parallel", "parallel", "arbitrary"),
        name="mla_attn",
    )(q_t, k_cat, v_t, sg)


def _rotate_half_columns(w):
    half = ROPE_DIM // 2
    return jnp.concatenate([-w[..., half:], w[..., :half]], axis=-1)


def _rope_tables(s):
    inv = 1.0 / (ROPE_THETA ** (jnp.arange(0, ROPE_DIM, 2, dtype=F32) / ROPE_DIM))
    ang = jnp.arange(s, dtype=F32)[:, None] * inv[None, :]
    cos, sin = jnp.cos(ang), jnp.sin(ang)
    return jnp.tile(cos, (1, 2)), jnp.tile(sin, (1, 2))


def kernel(x, p, norm_pre, norm_post, w_in_a, w_out_a, w_in_b, q_latent_norm, w_uq, w_out_b,
           kv_norm, w_dkv, kv_latent_norm, w_uk, w_uv, w_ple_proj, w_ple_gate):
    b, s, d = x.shape
    t = b * s
    n_a, n_b = w_in_a.shape[0], w_in_b.shape[0]
    heads = d // HEAD_DIM
    q_rank = w_uq.shape[1]
    kv_rank = w_uk.shape[0]
    cos, sin = _rope_tables(s)

    x2d = x.reshape(t, d)
    p2d = p.reshape(p.shape[0], t, p.shape[-1])

    def finish_layer(layer, og, x2d, w_out):
        if layer + 1 < n_a:
            gains = [norm_pre[layer + 1]]
        elif layer + 1 == n_a and n_b:
            gains = [norm_pre[layer + 1], kv_norm]
        elif layer + 1 < n_a + n_b:
            gains = [norm_pre[layer + 1]]
        else:
            gains = []
        return _out_proj(og.reshape(t, -1), x2d, p2d[layer], w_out.astype(BF16),
                         norm_post[layer], w_ple_proj[layer].astype(BF16),
                         w_ple_gate[layer].astype(BF16), gains)

    h_next = None
    for i in range(n_a):
        qkvg = _proj_a(x2d, norm_pre[i], w_in_a[i])
        og = _sb_attention(qkvg.reshape(b, s, -1), heads)
        x2d, h_next = finish_layer(i, og, x2d, w_out_a[i])

    if n_b == 0:
        return x2d.reshape(b, s, d)
    if n_a == 0:
        raise NotImplementedError("a trunk without mixer-A layers is not supported")

    h_b, h_kv = h_next
    wd = jnp.concatenate([
        w_dkv[:, :kv_rank],
        jnp.pad(w_dkv[:, kv_rank:], ((0, 0), (0, 128 - ROPE_DIM))),
        jnp.pad(_rotate_half_columns(w_dkv[:, kv_rank:]), ((0, 0), (0, 128 - ROPE_DIM))),
    ], axis=1).astype(BF16)
    k_cat, v_t = _kv_side(h_kv.reshape(b, s, d), wd, kv_latent_norm, w_uk.astype(BF16),
                          w_uv.T.astype(BF16), cos, sin, heads)

    for j in range(n_b):
        layer = n_a + j
        wq = w_uq[j].reshape(q_rank, heads, QK_DIM)
        w_n_t = wq[:, :, :HEAD_DIM].reshape(q_rank, heads * HEAD_DIM).T.astype(BF16)
        w_r_t = wq[:, :, HEAD_DIM:].reshape(q_rank, heads * ROPE_DIM).T.astype(BF16)
        w_rot_t = _rotate_half_columns(wq[:, :, HEAD_DIM:]).reshape(
            q_rank, heads * ROPE_DIM).T.astype(BF16)
        q_t, sg = _q_side(h_b.reshape(b, s, d), w_in_b[j].astype(BF16), q_latent_norm[j],
                          w_n_t, w_r_t, w_rot_t, cos.T, sin.T, heads)
        og = _mla_attention(q_t, k_cat, v_t, sg)
        x2d, h_next = finish_layer(layer, og, x2d, w_out_b[j])
        if h_next:
            h_b = h_next[0]
    return x2d.reshape(b, s, d)
```

```python
import functools
import math

import jax
import jax.numpy as jnp
from jax import lax
from jax.experimental import pallas as pl
from jax.experimental.pallas import tpu as pltpu

RMS_EPS = 1e-6
NEG_INF = -1e30
HEAD_DIM = 128
ROPE_DIM = 64
QK_DIM = HEAD_DIM + ROPE_DIM
CHUNK = 64
HEADS_PER_STEP = 2
SCORES_AHEAD = 2
ROPE_THETA = 10000.0
LOG2_E = math.log2(math.e)
EXP2_UNDERFLOW = -160.0

V7X_VMEM_LIMIT_BYTES = 56 * 1024 * 1024
BF16 = jnp.bfloat16
F32 = jnp.float32


def _tile(n, pref):
    t = min(n, pref)
    while n % t:
        t //= 2
    return t


def _params(*sem):
    return pltpu.CompilerParams(dimension_semantics=sem,
                                vmem_limit_bytes=V7X_VMEM_LIMIT_BYTES)


def _resident(shape):
    nd = len(shape)
    return pl.BlockSpec(shape, lambda *_: (0,) * nd, pipeline_mode=pl.Buffered(1))


def _rms_scale(v):
    return lax.rsqrt(jnp.mean(v * v, axis=-1, keepdims=True) + RMS_EPS)


def _sigmoid(v):
    return 1.0 / (1.0 + jnp.exp(-v))


def _dot(a, b):
    return jnp.dot(a, b, preferred_element_type=F32)


def _dot_nt(a, b):
    return lax.dot_general(a, b, (((1,), (1,)), ((), ())), preferred_element_type=F32)


def _proj_a_kernel(x_ref, g_ref, w_ref, o_ref, h_scr, *, q_blocks, q_scale):
    j = pl.program_id(1)

    @pl.when(j == 0)
    def _():
        x = x_ref[...]
        h_scr[...] = (x * _rms_scale(x) * g_ref[...]).astype(BF16)

    acc = _dot(h_scr[...], w_ref[...].astype(BF16))

    @pl.when(j < q_blocks)
    def _():
        o_ref[...] = (acc * q_scale).astype(BF16)

    @pl.when((j >= q_blocks) & (j < 3 * q_blocks))
    def _():
        o_ref[...] = acc.astype(BF16)

    @pl.when(j >= 3 * q_blocks)
    def _():
        o_ref[...] = (acc * _sigmoid(acc)).astype(BF16)


def _proj_a(x2d, gain, w_in):
    t, d = x2d.shape
    n = w_in.shape[1]
    width = n // 4
    tm = _tile(t, 1024)
    tn = _tile(width, 1024)
    return pl.pallas_call(
        functools.partial(_proj_a_kernel, q_blocks=width // tn,
                          q_scale=HEAD_DIM ** -0.5 * LOG2_E),
        out_shape=jax.ShapeDtypeStruct((t, n), BF16),
        grid=(t // tm, n // tn),
        in_specs=[
            pl.BlockSpec((tm, d), lambda i, j: (i, 0)),
            pl.BlockSpec((1, d), lambda i, j: (0, 0)),
            pl.BlockSpec((d, tn), lambda i, j: (0, j)),
        ],
        out_specs=pl.BlockSpec((tm, tn), lambda i, j: (i, j)),
        scratch_shapes=[pltpu.VMEM((tm, d), BF16)],
        compiler_params=_params("parallel", "arbitrary"),
        name="proj_a",
    )(x2d, gain.reshape(1, d), w_in)


def _sb_attn_kernel(q_ref, k_ref, v_ref, sg_ref, upper_ref, o_ref, vt_scr, acc_scr, carry_scr,
                    *, tq, tk):
    i = pl.program_id(2)
    heads = []
    for hh in range(HEADS_PER_STEP):
        cols = pl.ds(hh * HEAD_DIM, HEAD_DIM)
        heads.append(functools.partial(
            _sb_head, q_ref.at[:, cols], k_ref.at[:, cols], v_ref.at[:, cols], sg_ref.at[:, cols],
            upper_ref, o_ref.at[:, cols], vt_scr.at[hh], acc_scr.at[hh], carry_scr.at[hh],
            i=i, tq=tq, tk=tk))

    @pl.when(i == 0)
    def _():
        for head in heads:
            head(first_step=True)

    @pl.when(i > 0)
    def _():
        for head in heads:
            head(first_step=False)


def _sb_head(q_ref, k_ref, v_ref, sg_ref, upper_ref, o_ref, vt_scr, acc_scr, carry_scr,
             *, i, tq, tk, first_step):
    if first_step:
        for c in range(vt_scr.shape[0]):
            vt_scr[c] = v_ref[c * tk:(c + 1) * tk, :].T

    q_t = q_ref[...].T
    upper = upper_ref[...]

    def scores(blk, half):
        start = pl.multiple_of(blk * tk, tk)
        return _dot(k_ref[pl.ds(start, tk), :], q_t[:, half * tk:(half + 1) * tk])

    def stage(z, strict):
        soft = jnp.log2(1.0 + jnp.exp2(-jnp.abs(z)))
        log_beta = jnp.minimum(z, 0.0) - soft
        log_keep = log_beta - z
        if strict is not None:
            log_keep = jnp.where(strict, log_keep, 0.0)
        later = _dot(upper, log_keep.astype(BF16))
        return log_beta + later, jnp.sum(log_keep, axis=0, keepdims=True)

    def group(chains, carry, acc):
        raw = [scores(blk, half) for blk, _, half, _ in chains[:SCORES_AHEAD]]
        staged = []
        for n, (_, strict, _, _) in enumerate(chains):
            if n + SCORES_AHEAD < len(chains):
                blk, _, half, _ = chains[n + SCORES_AHEAD]
                raw.append(scores(blk, half))
            staged.append(stage(raw[n], strict))
        carry, acc = list(carry), list(acc)
        for (blk, strict, half, valid), (pre, total) in zip(chains, staged):
            w = jnp.exp2(pre + carry[half])
            if strict is not None:
                w = jnp.where(strict, w, 0.0)
            if valid is not None:
                w = jnp.where(valid, w, 0.0)
                total = jnp.where(valid, total, 0.0)
            acc[half] = acc[half] + _dot(vt_scr[blk], w.astype(BF16))
            carry[half] = carry[half] + total
        acc_scr[...] = jnp.concatenate(acc, axis=1)
        carry_scr[...] = jnp.concatenate(carry, axis=1)

    key = lax.broadcasted_iota(jnp.int32, (tk, tk), 0)
    qry = lax.broadcasted_iota(jnp.int32, (tk, tk), 1)
    strict = key < qry
    first = 2 * i
    zeros = ([jnp.zeros((1, tk), F32)] * 2, [jnp.zeros((HEAD_DIM, tk), F32)] * 2)

    if first_step:
        group([(1, strict, 1, None), (0, strict, 0, None), (0, None, 1, None)], *zeros)
    else:
        group([(first + 1, strict, 1, None), (first, strict, 0, None),
               (first, None, 1, None), (first - 1, None, 0, None)], *zeros)

    def live(blk):
        return (blk >= 0) & (jnp.max(carry_scr[...]) > EXP2_UNDERFLOW)

    def body(blk):
        carry, acc = carry_scr[...], acc_scr[...]
        group([(blk, None, 1, None), (jnp.maximum(blk - 1, 0), None, 0, blk >= 1)],
              [carry[:, :tk], carry[:, tk:]], [acc[:, :tk], acc[:, tk:]])
        return blk - 1

    lax.while_loop(live, body, first - 1)
    o_ref[...] = (acc_scr[...].T * sg_ref[...].astype(F32)).astype(BF16)


def _sb_attention(qkvg, heads):
    b, s, _ = qkvg.shape
    tq = _tile(s, 512)
    tk = _tile(s, 256)
    assert tq == 2 * tk, "the kernel walks the query block as two halves of tk queries"
    groups, width = heads // HEADS_PER_STEP, HEADS_PER_STEP * HEAD_DIM
    upper = (lax.broadcasted_iota(jnp.int32, (tk, tk), 1)
             > lax.broadcasted_iota(jnp.int32, (tk, tk), 0)).astype(BF16)
    return pl.pallas_call(
        functools.partial(_sb_attn_kernel, tq=tq, tk=tk),
        out_shape=jax.ShapeDtypeStruct((b, s, heads * HEAD_DIM), BF16),
        grid=(b, groups, s // tq),
        in_specs=[
            pl.BlockSpec((None, tq, width), lambda bi, h, i: (bi, i, h)),
            pl.BlockSpec((None, s, width), lambda bi, h, i: (bi, 0, groups + h)),
            pl.BlockSpec((None, s, width), lambda bi, h, i: (bi, 0, 2 * groups + h)),
            pl.BlockSpec((None, tq, width), lambda bi, h, i: (bi, i, 3 * groups + h)),
            pl.BlockSpec((tk, tk), lambda bi, h, i: (0, 0)),
        ],
        out_specs=pl.BlockSpec((None, tq, width), lambda bi, h, i: (bi, i, h)),
        scratch_shapes=[pltpu.VMEM((HEADS_PER_STEP, s // tk, HEAD_DIM, tk), BF16),
                        pltpu.VMEM((HEADS_PER_STEP, HEAD_DIM, tq), F32),
                        pltpu.VMEM((HEADS_PER_STEP, 1, tq), F32)],
        compiler_params=_params("parallel", "parallel", "arbitrary"),
        name="sb_attn",
    )(qkvg, qkvg, qkvg, qkvg, upper)


def _out_kernel(og_ref, x_ref, p_ref, wout_ref, gpost_ref, wpp_ref, wpg_ref, gnext_ref,
                xo_ref, *next_refs):
    y = _dot(og_ref[...], wout_ref[...])
    x1 = x_ref[...] + y * _rms_scale(y) * gpost_ref[...]
    gate = _dot(x1.astype(BF16), wpg_ref[...])
    emb = _dot(p_ref[...].astype(BF16), wpp_ref[...])
    x2 = x1 + emb * _sigmoid(gate)
    xo_ref[...] = x2
    if next_refs:
        normed = x2 * _rms_scale(x2)
        for n, ref in enumerate(next_refs):
            ref[...] = (normed * gnext_ref[n:n + 1, :]).astype(BF16)


def _out_proj(og2d, x2d, p3d, layer, w_out, g_post, w_pp, w_pg, next_gains):
    t, d = x2d.shape
    width = og2d.shape[1]
    ple = p3d.shape[2]
    tm = _tile(t, 512)
    n_next = len(next_gains)
    gnext = jnp.stack(next_gains) if n_next else jnp.zeros((1, d), F32)
    row = lambda i: (i, 0)
    outs = pl.pallas_call(
        _out_kernel,
        out_shape=[jax.ShapeDtypeStruct((t, d), F32)]
        + [jax.ShapeDtypeStruct((t, d), BF16)] * n_next,
        grid=(t // tm,),
        in_specs=[
            pl.BlockSpec((tm, width), row),
            pl.BlockSpec((tm, d), row),
            pl.BlockSpec((None, tm, ple), lambda i: (layer, i, 0)),
            _resident((width, d)),
            _resident((1, d)),
            _resident((ple, d)),
            _resident((d, d)),
            _resident(gnext.shape),
        ],
        out_specs=[pl.BlockSpec((tm, d), row)] * (1 + n_next),
        compiler_params=_params("parallel"),
        name="out_proj",
    )(og2d, x2d, p3d, w_out, g_post.reshape(1, d), w_pp, w_pg, gnext)
    return outs[0], list(outs[1:])


def _kv_kernel(h_ref, wd_ref, gl_ref, wuk_ref, wuvt_ref, cos_ref, sin_ref, k_ref, vt_ref,
               *, rank, heads):
    c = _dot(h_ref[...], wd_ref[...])
    ckv = c[:, :rank]
    k_rope = c[:, rank:rank + ROPE_DIM]
    k_rot = c[:, rank + 128:rank + 128 + ROPE_DIM]
    latent = (ckv * _rms_scale(ckv) * gl_ref[...]).astype(BF16)
    k_nope = _dot(latent, wuk_ref[...]).astype(BF16)
    vt_ref[...] = _dot_nt(wuvt_ref[...], latent).astype(BF16)
    roped = (k_rope * cos_ref[...] + k_rot * sin_ref[...]).astype(BF16)
    for h in range(heads):
        k_ref[h, :, 0:HEAD_DIM] = k_nope[:, h * HEAD_DIM:(h + 1) * HEAD_DIM]
        k_ref[h, :, HEAD_DIM:QK_DIM] = roped


def _kv_side(hkv, wd, g_latent, w_uk, w_uv_t, cos, sin, heads):
    b, s, d = hkv.shape
    rank = w_uk.shape[0]
    tm = _tile(s, 512)
    row = lambda bi, i: (bi, i, 0)
    return pl.pallas_call(
        functools.partial(_kv_kernel, rank=rank, heads=heads),
        out_shape=[jax.ShapeDtypeStruct((b, heads, s, QK_DIM), BF16),
                   jax.ShapeDtypeStruct((b, heads * HEAD_DIM, s), BF16)],
        grid=(b, s // tm),
        in_specs=[
            pl.BlockSpec((None, tm, d), row),
            _resident(wd.shape),
            _resident((1, rank)),
            _resident(w_uk.shape),
            _resident(w_uv_t.shape),
            pl.BlockSpec((tm, ROPE_DIM), lambda bi, i: (i, 0)),
            pl.BlockSpec((tm, ROPE_DIM), lambda bi, i: (i, 0)),
        ],
        out_specs=[pl.BlockSpec((None, heads, tm, QK_DIM), lambda bi, i: (bi, 0, i, 0)),
                   pl.BlockSpec((None, heads * HEAD_DIM, tm), lambda bi, i: (bi, 0, i))],
        compiler_params=_params("parallel", "parallel"),
        name="kv_side",
    )(hkv, wd, g_latent.reshape(1, rank), w_uk, w_uv_t, cos, sin)


def _q_kernel(h_ref, win_ref, gq_ref, wnt_ref, wrt_ref, wrott_ref, cost_ref, sint_ref,
              qt_ref, sg_ref, *, rank, heads, scale):
    c = _dot(h_ref[...], win_ref[...])
    cq = c[:, :rank]
    gate = c[:, rank:]
    sg_ref[...] = (gate * _sigmoid(gate)).astype(BF16)
    latent = (cq * _rms_scale(cq) * gq_ref[...]).astype(BF16)
    nope_t = (_dot_nt(wnt_ref[...], latent) * scale).astype(BF16)
    cos_t = jnp.tile(cost_ref[...], (heads, 1))
    sin_t = jnp.tile(sint_ref[...], (heads, 1))
    roped_t = ((_dot_nt(wrt_ref[...], latent) * cos_t + _dot_nt(wrott_ref[...], latent) * sin_t)
               * scale).astype(BF16)
    for h in range(heads):
        qt_ref[h, 0:HEAD_DIM, :] = nope_t[h * HEAD_DIM:(h + 1) * HEAD_DIM, :]
        qt_ref[h, HEAD_DIM:QK_DIM, :] = roped_t[h * ROPE_DIM:(h + 1) * ROPE_DIM, :]


def _q_side(h, w_in, g_latent, w_n_t, w_r_t, w_rot_t, cos_t, sin_t, heads):
    b, s, d = h.shape
    rank = w_n_t.shape[1]
    width = w_in.shape[1] - rank
    tm = _tile(s, 512)
    row = lambda bi, i: (bi, i, 0)
    return pl.pallas_call(
        functools.partial(_q_kernel, rank=rank, heads=heads, scale=QK_DIM ** -0.5 * LOG2_E),
        out_shape=[jax.ShapeDtypeStruct((b, heads, QK_DIM, s), BF16),
                   jax.ShapeDtypeStruct((b, s, width), BF16)],
        grid=(b, s // tm),
        in_specs=[
            pl.BlockSpec((None, tm, d), row),
            _resident(w_in.shape),
            _resident((1, rank)),
            _resident(w_n_t.shape),
            _resident(w_r_t.shape),
            _resident(w_rot_t.shape),
            pl.BlockSpec((ROPE_DIM, tm), lambda bi, i: (0, i)),
            pl.BlockSpec((ROPE_DIM, tm), lambda bi, i: (0, i)),
        ],
        out_specs=[pl.BlockSpec((None, heads, QK_DIM, tm), lambda bi, i: (bi, 0, 0, i)),
                   pl.BlockSpec((None, tm, width), row)],
        compiler_params=_params("parallel", "parallel"),
        name="q_side",
    )(h, w_in, g_latent.reshape(1, rank), w_n_t, w_r_t, w_rot_t, cos_t, sin_t)


def _mla_attn_kernel(qt_ref, k_ref, vt_ref, sg_ref, o_ref, s_scr, p_scr, acc_scr, *, tq, tk):
    i = pl.program_id(2)
    heads = []
    for hh in range(HEADS_PER_STEP):
        cols = pl.ds(hh * HEAD_DIM, HEAD_DIM)
        heads.append(functools.partial(
            _mla_head, qt_ref.at[hh], k_ref.at[hh], vt_ref.at[cols, :], sg_ref.at[:, cols],
            o_ref.at[:, cols], s_scr.at[hh], p_scr.at[hh], acc_scr.at[hh], i=i, tq=tq, tk=tk))

    @pl.when(i == 0)
    def _():
        for head in heads:
            head(first_step=True)

    @pl.when(i > 0)
    def _():
        for head in heads:
            head(first_step=False)


def _mla_head(qt_ref, k_ref, vt_ref, sg_ref, o_ref, s_scr, p_scr, acc_scr,
              *, i, tq, tk, first_step):
    nblk = (i + 1) * (tq // tk)
    q_t = qt_ref[...]
    acc_scr[...] = jnp.zeros_like(acc_scr)

    def key_start(n):
        return pl.multiple_of(n * tk, tk)

    def step(n, parity, stats, run, *, merge, softmax, score, mask=None, steady=False):
        if merge and steady:
            out = _dot(vt_ref[:, pl.ds(key_start(n - 2), tk)], p_scr[parity])
        if score:
            s_scr[parity] = _dot(k_ref[pl.ds(key_start(n), tk), :], q_t)
        if merge and not steady:
            out = _dot(vt_ref[:, pl.ds(key_start(n - 2), tk)], p_scr[parity])
        m_c, l_c = stats
        if softmax:
            s = s_scr[1 - parity]
            if mask is not None:
                s = jnp.where(mask, s, NEG_INF)
            m_next = jnp.max(s, axis=0, keepdims=True)
            prob = jnp.exp2(s - m_next)
            p_scr[1 - parity] = prob.astype(BF16)
            stats = (m_next, jnp.sum(prob, axis=0, keepdims=True))
        if merge:
            m_run, l_run = run
            m_new = jnp.maximum(m_run, m_c)
            alpha = jnp.exp2(m_run - m_new)
            weight = jnp.exp2(m_c - m_new)
            acc_scr[...] = acc_scr[...] * alpha + out * weight
            run = (m_new, l_run * alpha + l_c * weight)
        return stats, run

    def finish(run):
        out = (acc_scr[...] / run[1]).T
        o_ref[...] = (out * sg_ref[...].astype(F32)).astype(BF16)

    kchunk = lax.broadcasted_iota(jnp.int32, (tk, tq), 0) // CHUNK
    qchunk = lax.broadcasted_iota(jnp.int32, (tk, tq), 1) // CHUNK
    first_diag = kchunk <= qchunk
    second_diag = (kchunk + tk // CHUNK) <= qchunk
    row = jnp.zeros((1, tq), F32)
    state = ((row, row), (jnp.full((1, tq), NEG_INF, F32), row))
    everything = dict(merge=True, softmax=True, score=True)

    if first_step:
        st = step(0, 0, *state, merge=False, softmax=False, score=True)
        st = step(1, 1, *st, merge=False, softmax=True, score=True, mask=first_diag)
        st = step(2, 0, *st, merge=True, softmax=True, score=False, mask=second_diag)
        st = step(3, 1, *st, merge=True, softmax=False, score=False)
        finish(st[1])
    else:
        st = step(0, 0, *state, merge=False, softmax=False, score=True)
        st = step(1, 1, *st, merge=False, softmax=True, score=True)
        st = step(2, 0, *st, **everything)

        def pair(t, st):
            st = step(3 + 2 * t, 1, *st, steady=True, **everything)
            return step(4 + 2 * t, 0, *st, steady=True, **everything)

        st = lax.fori_loop(0, (nblk - 4) // 2, pair, st)
        st = step(nblk - 1, 1, *st, mask=first_diag, **everything)
        st = step(nblk, 0, *st, merge=True, softmax=True, score=False, mask=second_diag)
        st = step(nblk + 1, 1, *st, merge=True, softmax=False, score=False)
        finish(st[1])


def _mla_attention(q_t, k_cat, v_t, sg):
    b, heads, _, s = q_t.shape
    tq = _tile(s, 512)
    tk = _tile(s, 256)
    assert tq == 2 * tk, "the pipeline below peels exactly two diagonal key blocks"
    width = HEADS_PER_STEP * HEAD_DIM
    return pl.pallas_call(
        functools.partial(_mla_attn_kernel, tq=tq, tk=tk),
        out_shape=jax.ShapeDtypeStruct((b, s, heads * HEAD_DIM), BF16),
        grid=(b, heads // HEADS_PER_STEP, s // tq),
        in_specs=[
            pl.BlockSpec((None, HEADS_PER_STEP, QK_DIM, tq), lambda bi, h, i: (bi, h, 0, i)),
            pl.BlockSpec((None, HEADS_PER_STEP, s, QK_DIM), lambda bi, h, i: (bi, h, 0, 0)),
            pl.BlockSpec((None, width, s), lambda bi, h, i: (bi, h, 0)),
            pl.BlockSpec((None, tq, width), lambda bi, h, i: (bi, i, h)),
        ],
        out_specs=pl.BlockSpec((None, tq, width), lambda bi, h, i: (bi, i, h)),
        scratch_shapes=[pltpu.VMEM((HEADS_PER_STEP, 2, tk, tq), F32),
                        pltpu.VMEM((HEADS_PER_STEP, 2, tk, tq), BF16),
                        pltpu.VMEM((HEADS_PER_STEP, HEAD_DIM, tq), F32)],
        compiler_params=_params("parallel", "parallel", "arbitrary"),
        name="mla_attn",
    )(q_t, k_cat, v_t, sg)


def _rotate_half_columns(w):
    half = ROPE_DIM // 2
    return jnp.concatenate([-w[..., half:], w[..., :half]], axis=-1)


def _rope_tables(s):
    inv = 1.0 / (ROPE_THETA ** (jnp.arange(0, ROPE_DIM, 2, dtype=F32) / ROPE_DIM))
    ang = jnp.arange(s, dtype=F32)[:, None] * inv[None, :]
    cos, sin = jnp.cos(ang), jnp.sin(ang)
    return jnp.tile(cos, (1, 2)), jnp.tile(sin, (1, 2))


def kernel(x, p, norm_pre, norm_post, w_in_a, w_out_a, w_in_b, q_latent_norm, w_uq, w_out_b,
           kv_norm, w_dkv, kv_latent_norm, w_uk, w_uv, w_ple_proj, w_ple_gate):
    b, s, d = x.shape
    t = b * s
    n_a, n_b = w_in_a.shape[0], w_in_b.shape[0]
    heads = d // HEAD_DIM
    q_rank = w_uq.shape[1]
    kv_rank = w_uk.shape[0]
    cos, sin = _rope_tables(s)

    x2d = x.reshape(t, d)
    p3d = p.reshape(p.shape[0], t, p.shape[-1])

    def finish_layer(layer, og, x2d, w_out):
        if layer + 1 < n_a:
            gains = [norm_pre[layer + 1]]
        elif layer + 1 == n_a and n_b:
            gains = [norm_pre[layer + 1], kv_norm]
        elif layer + 1 < n_a + n_b:
            gains = [norm_pre[layer + 1]]
        else:
            gains = []
        return _out_proj(og.reshape(t, -1), x2d, p3d, layer, w_out.astype(BF16),
                         norm_post[layer], w_ple_proj[layer].astype(BF16),
                         w_ple_gate[layer].astype(BF16), gains)

    h_next = None
    for i in range(n_a):
        qkvg = _proj_a(x2d, norm_pre[i], w_in_a[i])
        og = _sb_attention(qkvg.reshape(b, s, -1), heads)
        x2d, h_next = finish_layer(i, og, x2d, w_out_a[i])

    if n_b == 0:
        return x2d.reshape(b, s, d)
    if n_a == 0:
        raise NotImplementedError("a trunk without mixer-A layers is not supported")

    h_b, h_kv = h_next
    wd = jnp.concatenate([
        w_dkv[:, :kv_rank],
        jnp.pad(w_dkv[:, kv_rank:], ((0, 0), (0, 128 - ROPE_DIM))),
        jnp.pad(_rotate_half_columns(w_dkv[:, kv_rank:]), ((0, 0), (0, 128 - ROPE_DIM))),
    ], axis=1).astype(BF16)
    k_cat, v_t = _kv_side(h_kv.reshape(b, s, d), wd, kv_latent_norm, w_uk.astype(BF16),
                          w_uv.T.astype(BF16), cos, sin, heads)

    for j in range(n_b):
        layer = n_a + j
        wq = w_uq[j].reshape(q_rank, heads, QK_DIM)
        w_n_t = wq[:, :, :HEAD_DIM].reshape(q_rank, heads * HEAD_DIM).T.astype(BF16)
        w_r_t = wq[:, :, HEAD_DIM:].reshape(q_rank, heads * ROPE_DIM).T.astype(BF16)
        w_rot_t = _rotate_half_columns(wq[:, :, HEAD_DIM:]).reshape(
            q_rank, heads * ROPE_DIM).T.astype(BF16)
        q_t, sg = _q_side(h_b.reshape(b, s, d), w_in_b[j].astype(BF16), q_latent_norm[j],
                          w_n_t, w_r_t, w_rot_t, cos.T, sin.T, heads)
        og = _mla_attention(q_t, k_cat, v_t, sg)
        x2d, h_next = finish_layer(layer, og, x2d, w_out_b[j])
        if h_next:
            h_b = h_next[0]
    return x2d.reshape(b, s, d)
```

```python
import functools
import math

import jax
import jax.numpy as jnp
from jax import lax
from jax.experimental import pallas as pl
from jax.experimental.pallas import tpu as pltpu

RMS_EPS = 1e-6
NEG_INF = -1e30
HEAD_DIM = 128
ROPE_DIM = 64
QK_DIM = HEAD_DIM + ROPE_DIM
CHUNK = 64
HEADS_PER_STEP = 4
SCORES_AHEAD = 2
ROPE_THETA = 10000.0
LOG2_E = math.log2(math.e)
EXP2_UNDERFLOW = -160.0

V7X_VMEM_LIMIT_BYTES = 56 * 1024 * 1024
BF16 = jnp.bfloat16
F32 = jnp.float32


def _tile(n, pref):
    t = min(n, pref)
    while n % t:
        t //= 2
    return t


def _params(*sem):
    return pltpu.CompilerParams(dimension_semantics=sem,
                                vmem_limit_bytes=V7X_VMEM_LIMIT_BYTES)


def _resident(shape):
    nd = len(shape)
    return pl.BlockSpec(shape, lambda *_: (0,) * nd, pipeline_mode=pl.Buffered(1))


def _rms_scale(v):
    return lax.rsqrt(jnp.mean(v * v, axis=-1, keepdims=True) + RMS_EPS)


def _sigmoid(v):
    return 1.0 / (1.0 + jnp.exp(-v))


def _dot(a, b):
    return jnp.dot(a, b, preferred_element_type=F32)


def _dot_nt(a, b):
    return lax.dot_general(a, b, (((1,), (1,)), ((), ())), preferred_element_type=F32)


def _proj_a_kernel(x_ref, g_ref, w_ref, o_ref, h_scr, *, q_blocks, q_scale):
    j = pl.program_id(1)

    @pl.when(j == 0)
    def _():
        x = x_ref[...]
        h_scr[...] = (x * _rms_scale(x) * g_ref[...]).astype(BF16)

    acc = _dot(h_scr[...], w_ref[...].astype(BF16))

    @pl.when(j < q_blocks)
    def _():
        o_ref[...] = (acc * q_scale).astype(BF16)

    @pl.when((j >= q_blocks) & (j < 3 * q_blocks))
    def _():
        o_ref[...] = acc.astype(BF16)

    @pl.when(j >= 3 * q_blocks)
    def _():
        o_ref[...] = (acc * _sigmoid(acc)).astype(BF16)


def _proj_a(x2d, gain, w_in):
    t, d = x2d.shape
    n = w_in.shape[1]
    width = n // 4
    tm = _tile(t, 1024)
    tn = _tile(width, 1024)
    return pl.pallas_call(
        functools.partial(_proj_a_kernel, q_blocks=width // tn,
                          q_scale=HEAD_DIM ** -0.5 * LOG2_E),
        out_shape=jax.ShapeDtypeStruct((t, n), BF16),
        grid=(t // tm, n // tn),
        in_specs=[
            pl.BlockSpec((tm, d), lambda i, j: (i, 0)),
            pl.BlockSpec((1, d), lambda i, j: (0, 0)),
            pl.BlockSpec((d, tn), lambda i, j: (0, j)),
        ],
        out_specs=pl.BlockSpec((tm, tn), lambda i, j: (i, j)),
        scratch_shapes=[pltpu.VMEM((tm, d), BF16)],
        compiler_params=_params("parallel", "arbitrary"),
        name="proj_a",
    )(x2d, gain.reshape(1, d), w_in)


def _sb_attn_kernel(q_ref, k_ref, v_ref, sg_ref, upper_ref, o_ref, vt_scr, acc_scr, carry_scr,
                    *, tq, tk):
    i = pl.program_id(2)
    heads = []
    for hh in range(HEADS_PER_STEP):
        cols = pl.ds(hh * HEAD_DIM, HEAD_DIM)
        heads.append(functools.partial(
            _sb_head, q_ref.at[:, cols], k_ref.at[:, cols], v_ref.at[:, cols], sg_ref.at[:, cols],
            upper_ref, o_ref.at[:, cols], vt_scr.at[hh], acc_scr.at[hh], carry_scr.at[hh],
            i=i, tq=tq, tk=tk))

    @pl.when(i == 0)
    def _():
        for head in heads:
            head(first_step=True)

    @pl.when(i > 0)
    def _():
        for head in heads:
            head(first_step=False)


def _sb_head(q_ref, k_ref, v_ref, sg_ref, upper_ref, o_ref, vt_scr, acc_scr, carry_scr,
             *, i, tq, tk, first_step):
    if first_step:
        for c in range(vt_scr.shape[0]):
            vt_scr[c] = v_ref[c * tk:(c + 1) * tk, :].T

    q_t = q_ref[...].T
    upper = upper_ref[...]

    def scores(blk, half):
        start = pl.multiple_of(blk * tk, tk)
        return _dot(k_ref[pl.ds(start, tk), :], q_t[:, half * tk:(half + 1) * tk])

    def stage(z, strict):
        soft = jnp.log2(1.0 + jnp.exp2(-jnp.abs(z)))
        log_beta = jnp.minimum(z, 0.0) - soft
        log_keep = log_beta - z
        if strict is not None:
            log_keep = jnp.where(strict, log_keep, 0.0)
        later = _dot(upper, log_keep.astype(BF16))
        return log_beta + later, jnp.sum(log_keep, axis=0, keepdims=True)

    def group(chains, carry, acc):
        raw = [scores(blk, half) for blk, _, half, _ in chains[:SCORES_AHEAD]]
        staged = []
        for n, (_, strict, _, _) in enumerate(chains):
            if n + SCORES_AHEAD < len(chains):
                blk, _, half, _ = chains[n + SCORES_AHEAD]
                raw.append(scores(blk, half))
            staged.append(stage(raw[n], strict))
        carry, acc = list(carry), list(acc)
        for (blk, strict, half, valid), (pre, total) in zip(chains, staged):
            w = jnp.exp2(pre + carry[half])
            if strict is not None:
                w = jnp.where(strict, w, 0.0)
            if valid is not None:
                w = jnp.where(valid, w, 0.0)
                total = jnp.where(valid, total, 0.0)
            acc[half] = acc[half] + _dot(vt_scr[blk], w.astype(BF16))
            carry[half] = carry[half] + total
        acc_scr[...] = jnp.concatenate(acc, axis=1)
        carry_scr[...] = jnp.concatenate(carry, axis=1)

    key = lax.broadcasted_iota(jnp.int32, (tk, tk), 0)
    qry = lax.broadcasted_iota(jnp.int32, (tk, tk), 1)
    strict = key < qry
    first = 2 * i
    zeros = ([jnp.zeros((1, tk), F32)] * 2, [jnp.zeros((HEAD_DIM, tk), F32)] * 2)

    if first_step:
        group([(1, strict, 1, None), (0, strict, 0, None), (0, None, 1, None)], *zeros)
    else:
        group([(first + 1, strict, 1, None), (first, strict, 0, None),
               (first, None, 1, None), (first - 1, None, 0, None)], *zeros)

    def live(blk):
        return (blk >= 0) & (jnp.max(carry_scr[...]) > EXP2_UNDERFLOW)

    def body(blk):
        carry, acc = carry_scr[...], acc_scr[...]
        group([(blk, None, 1, None), (jnp.maximum(blk - 1, 0), None, 0, blk >= 1)],
              [carry[:, :tk], carry[:, tk:]], [acc[:, :tk], acc[:, tk:]])
        return blk - 1

    lax.while_loop(live, body, first - 1)
    o_ref[...] = (acc_scr[...].T * sg_ref[...].astype(F32)).astype(BF16)


def _sb_attention(qkvg, heads):
    b, s, _ = qkvg.shape
    tq = _tile(s, 512)
    tk = _tile(s, 256)
    assert tq == 2 * tk, "the kernel walks the query block as two halves of tk queries"
    groups, width = heads // HEADS_PER_STEP, HEADS_PER_STEP * HEAD_DIM
    upper = (lax.broadcasted_iota(jnp.int32, (tk, tk), 1)
             > lax.broadcasted_iota(jnp.int32, (tk, tk), 0)).astype(BF16)
    return pl.pallas_call(
        functools.partial(_sb_attn_kernel, tq=tq, tk=tk),
        out_shape=jax.ShapeDtypeStruct((b, s, heads * HEAD_DIM), BF16),
        grid=(b, groups, s // tq),
        in_specs=[
            pl.BlockSpec((None, tq, width), lambda bi, h, i: (bi, i, h)),
            pl.BlockSpec((None, s, width), lambda bi, h, i: (bi, 0, groups + h)),
            pl.BlockSpec((None, s, width), lambda bi, h, i: (bi, 0, 2 * groups + h)),
            pl.BlockSpec((None, tq, width), lambda bi, h, i: (bi, i, 3 * groups + h)),
            pl.BlockSpec((tk, tk), lambda bi, h, i: (0, 0)),
        ],
        out_specs=pl.BlockSpec((None, tq, width), lambda bi, h, i: (bi, i, h)),
        scratch_shapes=[pltpu.VMEM((HEADS_PER_STEP, s // tk, HEAD_DIM, tk), BF16),
                        pltpu.VMEM((HEADS_PER_STEP, HEAD_DIM, tq), F32),
                        pltpu.VMEM((HEADS_PER_STEP, 1, tq), F32)],
        compiler_params=_params("parallel", "parallel", "arbitrary"),
        name="sb_attn",
    )(qkvg, qkvg, qkvg, qkvg, upper)


def _out_kernel(og_ref, x_ref, p_ref, wout_ref, gpost_ref, wpp_ref, wpg_ref, gnext_ref,
                xo_ref, *next_refs):
    y = _dot(og_ref[...], wout_ref[...])
    x1 = x_ref[...] + y * _rms_scale(y) * gpost_ref[...]
    gate = _dot(x1.astype(BF16), wpg_ref[...])
    emb = _dot(p_ref[...].astype(BF16), wpp_ref[...])
    x2 = x1 + emb * _sigmoid(gate)
    xo_ref[...] = x2
    if next_refs:
        normed = x2 * _rms_scale(x2)
        for n, ref in enumerate(next_refs):
            ref[...] = (normed * gnext_ref[n:n + 1, :]).astype(BF16)


def _out_proj(og2d, x2d, p3d, layer, w_out, g_post, w_pp, w_pg, next_gains):
    t, d = x2d.shape
    width = og2d.shape[1]
    ple = p3d.shape[2]
    tm = _tile(t, 512)
    n_next = len(next_gains)
    gnext = jnp.stack(next_gains) if n_next else jnp.zeros((1, d), F32)
    row = lambda i: (i, 0)
    outs = pl.pallas_call(
        _out_kernel,
        out_shape=[jax.ShapeDtypeStruct((t, d), F32)]
        + [jax.ShapeDtypeStruct((t, d), BF16)] * n_next,
        grid=(t // tm,),
        in_specs=[
            pl.BlockSpec((tm, width), row),
            pl.BlockSpec((tm, d), row),
            pl.BlockSpec((None, tm, ple), lambda i: (layer, i, 0)),
            _resident((width, d)),
            _resident((1, d)),
            _resident((ple, d)),
            _resident((d, d)),
            _resident(gnext.shape),
        ],
        out_specs=[pl.BlockSpec((tm, d), row)] * (1 + n_next),
        compiler_params=_params("parallel"),
        name="out_proj",
    )(og2d, x2d, p3d, w_out, g_post.reshape(1, d), w_pp, w_pg, gnext)
    return outs[0], list(outs[1:])


def _kv_kernel(h_ref, wd_ref, gl_ref, wuk_ref, wuvt_ref, cos_ref, sin_ref, k_ref, vt_ref,
               *, rank, heads):
    c = _dot(h_ref[...], wd_ref[...])
    ckv = c[:, :rank]
    k_rope = c[:, rank:rank + ROPE_DIM]
    k_rot = c[:, rank + 128:rank + 128 + ROPE_DIM]
    latent = (ckv * _rms_scale(ckv) * gl_ref[...]).astype(BF16)
    k_nope = _dot(latent, wuk_ref[...]).astype(BF16)
    vt_ref[...] = _dot_nt(wuvt_ref[...], latent).astype(BF16)
    roped = (k_rope * cos_ref[...] + k_rot * sin_ref[...]).astype(BF16)
    for h in range(heads):
        k_ref[h, :, 0:HEAD_DIM] = k_nope[:, h * HEAD_DIM:(h + 1) * HEAD_DIM]
        k_ref[h, :, HEAD_DIM:QK_DIM] = roped


def _kv_side(hkv, wd, g_latent, w_uk, w_uv_t, cos, sin, heads):
    b, s, d = hkv.shape
    rank = w_uk.shape[0]
    tm = _tile(s, 512)
    row = lambda bi, i: (bi, i, 0)
    return pl.pallas_call(
        functools.partial(_kv_kernel, rank=rank, heads=heads),
        out_shape=[jax.ShapeDtypeStruct((b, heads, s, QK_DIM), BF16),
                   jax.ShapeDtypeStruct((b, heads * HEAD_DIM, s), BF16)],
        grid=(b, s // tm),
        in_specs=[
            pl.BlockSpec((None, tm, d), row),
            _resident(wd.shape),
            _resident((1, rank)),
            _resident(w_uk.shape),
            _resident(w_uv_t.shape),
            pl.BlockSpec((tm, ROPE_DIM), lambda bi, i: (i, 0)),
            pl.BlockSpec((tm, ROPE_DIM), lambda bi, i: (i, 0)),
        ],
        out_specs=[pl.BlockSpec((None, heads, tm, QK_DIM), lambda bi, i: (bi, 0, i, 0)),
                   pl.BlockSpec((None, heads * HEAD_DIM, tm), lambda bi, i: (bi, 0, i))],
        compiler_params=_params("parallel", "parallel"),
        name="kv_side",
    )(hkv, wd, g_latent.reshape(1, rank), w_uk, w_uv_t, cos, sin)


def _q_kernel(h_ref, win_ref, gq_ref, wnt_ref, wrt_ref, wrott_ref, cost_ref, sint_ref,
              qt_ref, sg_ref, *, rank, heads, scale):
    c = _dot(h_ref[...], win_ref[...])
    cq = c[:, :rank]
    gate = c[:, rank:]
    sg_ref[...] = (gate * _sigmoid(gate)).astype(BF16)
    latent = (cq * _rms_scale(cq) * gq_ref[...]).astype(BF16)
    nope_t = (_dot_nt(wnt_ref[...], latent) * scale).astype(BF16)
    cos_t = jnp.tile(cost_ref[...], (heads, 1))
    sin_t = jnp.tile(sint_ref[...], (heads, 1))
    roped_t = ((_dot_nt(wrt_ref[...], latent) * cos_t + _dot_nt(wrott_ref[...], latent) * sin_t)
               * scale).astype(BF16)
    for h in range(heads):
        qt_ref[h, 0:HEAD_DIM, :] = nope_t[h * HEAD_DIM:(h + 1) * HEAD_DIM, :]
        qt_ref[h, HEAD_DIM:QK_DIM, :] = roped_t[h * ROPE_DIM:(h + 1) * ROPE_DIM, :]


def _q_side(h, w_in, g_latent, w_n_t, w_r_t, w_rot_t, cos_t, sin_t, heads):
    b, s, d = h.shape
    rank = w_n_t.shape[1]
    width = w_in.shape[1] - rank
    tm = _tile(s, 512)
    row = lambda bi, i: (bi, i, 0)
    return pl.pallas_call(
        functools.partial(_q_kernel, rank=rank, heads=heads, scale=QK_DIM ** -0.5 * LOG2_E),
        out_shape=[jax.ShapeDtypeStruct((b, heads, QK_DIM, s), BF16),
                   jax.ShapeDtypeStruct((b, s, width), BF16)],
        grid=(b, s // tm),
        in_specs=[
            pl.BlockSpec((None, tm, d), row),
            _resident(w_in.shape),
            _resident((1, rank)),
            _resident(w_n_t.shape),
            _resident(w_r_t.shape),
            _resident(w_rot_t.shape),
            pl.BlockSpec((ROPE_DIM, tm), lambda bi, i: (0, i)),
            pl.BlockSpec((ROPE_DIM, tm), lambda bi, i: (0, i)),
        ],
        out_specs=[pl.BlockSpec((None, heads, QK_DIM, tm), lambda bi, i: (bi, 0, 0, i)),
                   pl.BlockSpec((None, tm, width), row)],
        compiler_params=_params("parallel", "parallel"),
        name="q_side",
    )(h, w_in, g_latent.reshape(1, rank), w_n_t, w_r_t, w_rot_t, cos_t, sin_t)


def _mla_attn_kernel(qt_ref, k_ref, vt_ref, sg_ref, o_ref, s_scr, p_scr, acc_scr, *, tq, tk):
    i = pl.program_id(2)
    heads = []
    for hh in range(HEADS_PER_STEP):
        cols = pl.ds(hh * HEAD_DIM, HEAD_DIM)
        heads.append(functools.partial(
            _mla_head, qt_ref.at[hh], k_ref.at[hh], vt_ref.at[cols, :], sg_ref.at[:, cols],
            o_ref.at[:, cols], s_scr.at[hh], p_scr.at[hh], acc_scr.at[hh], i=i, tq=tq, tk=tk))

    @pl.when(i == 0)
    def _():
        for head in heads:
            head(first_step=True)

    @pl.when(i > 0)
    def _():
        for head in heads:
            head(first_step=False)


def _mla_head(qt_ref, k_ref, vt_ref, sg_ref, o_ref, s_scr, p_scr, acc_scr,
              *, i, tq, tk, first_step):
    nblk = (i + 1) * (tq // tk)
    q_t = qt_ref[...]
    acc_scr[...] = jnp.zeros_like(acc_scr)

    def key_start(n):
        return pl.multiple_of(n * tk, tk)

    def step(n, parity, stats, run, *, merge, softmax, score, mask=None, steady=False):
        if merge and steady:
            out = _dot(vt_ref[:, pl.ds(key_start(n - 2), tk)], p_scr[parity])
        if score:
            s_scr[parity] = _dot(k_ref[pl.ds(key_start(n), tk), :], q_t)
        if merge and not steady:
            out = _dot(vt_ref[:, pl.ds(key_start(n - 2), tk)], p_scr[parity])
        m_c, l_c = stats
        if softmax:
            s = s_scr[1 - parity]
            if mask is not None:
                s = jnp.where(mask, s, NEG_INF)
            m_next = jnp.max(s, axis=0, keepdims=True)
            prob = jnp.exp2(s - m_next)
            p_scr[1 - parity] = prob.astype(BF16)
            stats = (m_next, jnp.sum(prob, axis=0, keepdims=True))
        if merge:
            m_run, l_run = run
            m_new = jnp.maximum(m_run, m_c)
            alpha = jnp.exp2(m_run - m_new)
            weight = jnp.exp2(m_c - m_new)
            acc_scr[...] = acc_scr[...] * alpha + out * weight
            run = (m_new, l_run * alpha + l_c * weight)
        return stats, run

    def finish(run):
        out = (acc_scr[...] / run[1]).T
        o_ref[...] = (out * sg_ref[...].astype(F32)).astype(BF16)

    kchunk = lax.broadcasted_iota(jnp.int32, (tk, tq), 0) // CHUNK
    qchunk = lax.broadcasted_iota(jnp.int32, (tk, tq), 1) // CHUNK
    first_diag = kchunk <= qchunk
    second_diag = (kchunk + tk // CHUNK) <= qchunk
    row = jnp.zeros((1, tq), F32)
    state = ((row, row), (jnp.full((1, tq), NEG_INF, F32), row))
    everything = dict(merge=True, softmax=True, score=True)

    if first_step:
        st = step(0, 0, *state, merge=False, softmax=False, score=True)
        st = step(1, 1, *st, merge=False, softmax=True, score=True, mask=first_diag)
        st = step(2, 0, *st, merge=True, softmax=True, score=False, mask=second_diag)
        st = step(3, 1, *st, merge=True, softmax=False, score=False)
        finish(st[1])
    else:
        st = step(0, 0, *state, merge=False, softmax=False, score=True)
        st = step(1, 1, *st, merge=False, softmax=True, score=True)
        st = step(2, 0, *st, **everything)

        def pair(t, st):
            st = step(3 + 2 * t, 1, *st, steady=True, **everything)
            return step(4 + 2 * t, 0, *st, steady=True, **everything)

        st = lax.fori_loop(0, (nblk - 4) // 2, pair, st)
        st = step(nblk - 1, 1, *st, mask=first_diag, **everything)
        st = step(nblk, 0, *st, merge=True, softmax=True, score=False, mask=second_diag)
        st = step(nblk + 1, 1, *st, merge=True, softmax=False, score=False)
        finish(st[1])


def _mla_attention(q_t, k_cat, v_t, sg):
    b, heads, _, s = q_t.shape
    tq = _tile(s, 512)
    tk = _tile(s, 256)
    assert tq == 2 * tk, "the pipeline below peels exactly two diagonal key blocks"
    width = HEADS_PER_STEP * HEAD_DIM
    return pl.pallas_call(
        functools.partial(_mla_attn_kernel, tq=tq, tk=tk),
        out_shape=jax.ShapeDtypeStruct((b, s, heads * HEAD_DIM), BF16),
        grid=(b, heads // HEADS_PER_STEP, s // tq),
        in_specs=[
            pl.BlockSpec((None, HEADS_PER_STEP, QK_DIM, tq), lambda bi, h, i: (bi, h, 0, i)),
            pl.BlockSpec((None, HEADS_PER_STEP, s, QK_DIM), lambda bi, h, i: (bi, h, 0, 0)),
            pl.BlockSpec((None, width, s), lambda bi, h, i: (bi, h, 0)),
            pl.BlockSpec((None, tq, width), lambda bi, h, i: (bi, i, h)),
        ],
        out_specs=pl.BlockSpec((None, tq, width), lambda bi, h, i: (bi, i, h)),
        scratch_shapes=[pltpu.VMEM((HEADS_PER_STEP, 2, tk, tq), F32),
                        pltpu.VMEM((HEADS_PER_STEP, 2, tk, tq), BF16),
                        pltpu.VMEM((HEADS_PER_STEP, HEAD_DIM, tq), F32)],
        compiler_params=_params("parallel", "parallel", "arbitrary"),
        name="mla_attn",
    )(q_t, k_cat, v_t, sg)


def _rotate_half_columns(w):
    half = ROPE_DIM // 2
    return jnp.concatenate([-w[..., half:], w[..., :half]], axis=-1)


def _rope_tables(s):
    inv = 1.0 / (ROPE_THETA ** (jnp.arange(0, ROPE_DIM, 2, dtype=F32) / ROPE_DIM))
    ang = jnp.arange(s, dtype=F32)[:, None] * inv[None, :]
    cos, sin = jnp.cos(ang), jnp.sin(ang)
    return jnp.tile(cos, (1, 2)), jnp.tile(sin, (1, 2))


def kernel(x, p, norm_pre, norm_post, w_in_a, w_out_a, w_in_b, q_latent_norm, w_uq, w_out_b,
           kv_norm, w_dkv, kv_latent_norm, w_uk, w_uv, w_ple_proj, w_ple_gate):
    b, s, d = x.shape
    t = b * s
    n_a, n_b = w_in_a.shape[0], w_in_b.shape[0]
    heads = d // HEAD_DIM
    q_rank = w_uq.shape[1]
    kv_rank = w_uk.shape[0]
    cos, sin = _rope_tables(s)

    x2d = x.reshape(t, d)
    p3d = p.reshape(p.shape[0], t, p.shape[-1])

    def finish_layer(layer, og, x2d, w_out):
        if layer + 1 < n_a:
            gains = [norm_pre[layer + 1]]
        elif layer + 1 == n_a and n_b:
            gains = [norm_pre[layer + 1], kv_norm]
        elif layer + 1 < n_a + n_b:
            gains = [norm_pre[layer + 1]]
        else:
            gains = []
        return _out_proj(og.reshape(t, -1), x2d, p3d, layer, w_out.astype(BF16),
                         norm_post[layer], w_ple_proj[layer].astype(BF16),
                         w_ple_gate[layer].astype(BF16), gains)

    h_next = None
    for i in range(n_a):
        qkvg = _proj_a(x2d, norm_pre[i], w_in_a[i])
        og = _sb_attention(qkvg.reshape(b, s, -1), heads)
        x2d, h_next = finish_layer(i, og, x2d, w_out_a[i])

    if n_b == 0:
        return x2d.reshape(b, s, d)
    if n_a == 0:
        raise NotImplementedError("a trunk without mixer-A layers is not supported")

    h_b, h_kv = h_next
    wd = jnp.concatenate([
        w_dkv[:, :kv_rank],
        jnp.pad(w_dkv[:, kv_rank:], ((0, 0), (0, 128 - ROPE_DIM))),
        jnp.pad(_rotate_half_columns(w_dkv[:, kv_rank:]), ((0, 0), (0, 128 - ROPE_DIM))),
    ], axis=1).astype(BF16)
    k_cat, v_t = _kv_side(h_kv.reshape(b, s, d), wd, kv_latent_norm, w_uk.astype(BF16),
                          w_uv.T.astype(BF16), cos, sin, heads)

    for j in range(n_b):
        layer = n_a + j
        wq = w_uq[j].reshape(q_rank, heads, QK_DIM)
        w_n_t = wq[:, :, :HEAD_DIM].reshape(q_rank, heads * HEAD_DIM).T.astype(BF16)
        w_r_t = wq[:, :, HEAD_DIM:].reshape(q_rank, heads * ROPE_DIM).T.astype(BF16)
        w_rot_t = _rotate_half_columns(wq[:, :, HEAD_DIM:]).reshape(
            q_rank, heads * ROPE_DIM).T.astype(BF16)
        q_t, sg = _q_side(h_b.reshape(b, s, d), w_in_b[j].astype(BF16), q_latent_norm[j],
                          w_n_t, w_r_t, w_rot_t, cos.T, sin.T, heads)
        og = _mla_attention(q_t, k_cat, v_t, sg)
        x2d, h_next = finish_layer(layer, og, x2d, w_out_b[j])
        if h_next:
            h_b = h_next[0]
    return x2d.reshape(b, s, d)
```

```python
import functools
import math

import jax
import jax.numpy as jnp
from jax import lax
from jax.experimental import pallas as pl
from jax.experimental.pallas import tpu as pltpu

RMS_EPS = 1e-6
NEG_INF = -1e30
HEAD_DIM = 128
ROPE_DIM = 64
QK_DIM = HEAD_DIM + ROPE_DIM
CHUNK = 64
HEADS_PER_STEP = 4
SB_HEADS_PER_STEP = 8
SCORES_AHEAD = 2
ROPE_THETA = 10000.0
LOG2_E = math.log2(math.e)
EXP2_UNDERFLOW = -160.0

V7X_VMEM_LIMIT_BYTES = 56 * 1024 * 1024
BF16 = jnp.bfloat16
F32 = jnp.float32


def _tile(n, pref):
    t = min(n, pref)
    while n % t:
        t //= 2
    return t


def _params(*sem):
    return pltpu.CompilerParams(dimension_semantics=sem,
                                vmem_limit_bytes=V7X_VMEM_LIMIT_BYTES)


def _resident(shape):
    nd = len(shape)
    return pl.BlockSpec(shape, lambda *_: (0,) * nd, pipeline_mode=pl.Buffered(1))


def _rms_scale(v):
    return lax.rsqrt(jnp.mean(v * v, axis=-1, keepdims=True) + RMS_EPS)


def _sigmoid(v):
    return 1.0 / (1.0 + jnp.exp(-v))


def _dot(a, b):
    return jnp.dot(a, b, preferred_element_type=F32)


def _dot_nt(a, b):
    return lax.dot_general(a, b, (((1,), (1,)), ((), ())), preferred_element_type=F32)


def _proj_a_kernel(x_ref, g_ref, w_ref, o_ref, h_scr, *, q_blocks, q_scale):
    j = pl.program_id(1)

    @pl.when(j == 0)
    def _():
        x = x_ref[...]
        h_scr[...] = (x * _rms_scale(x) * g_ref[...]).astype(BF16)

    acc = _dot(h_scr[...], w_ref[...].astype(BF16))

    @pl.when(j < q_blocks)
    def _():
        o_ref[...] = (acc * q_scale).astype(BF16)

    @pl.when((j >= q_blocks) & (j < 3 * q_blocks))
    def _():
        o_ref[...] = acc.astype(BF16)

    @pl.when(j >= 3 * q_blocks)
    def _():
        o_ref[...] = (acc * _sigmoid(acc)).astype(BF16)


def _proj_a(x2d, gain, w_in):
    t, d = x2d.shape
    n = w_in.shape[1]
    width = n // 4
    tm = _tile(t, 1024)
    tn = _tile(width, 1024)
    return pl.pallas_call(
        functools.partial(_proj_a_kernel, q_blocks=width // tn,
                          q_scale=HEAD_DIM ** -0.5 * LOG2_E),
        out_shape=jax.ShapeDtypeStruct((t, n), BF16),
        grid=(t // tm, n // tn),
        in_specs=[
            pl.BlockSpec((tm, d), lambda i, j: (i, 0)),
            pl.BlockSpec((1, d), lambda i, j: (0, 0)),
            pl.BlockSpec((d, tn), lambda i, j: (0, j)),
        ],
        out_specs=pl.BlockSpec((tm, tn), lambda i, j: (i, j)),
        scratch_shapes=[pltpu.VMEM((tm, d), BF16)],
        compiler_params=_params("parallel", "arbitrary"),
        name="proj_a",
    )(x2d, gain.reshape(1, d), w_in)


def _sb_attn_kernel(q_ref, k_ref, v_ref, sg_ref, upper_ref, o_ref, vt_scr, acc_scr, carry_scr,
                    *, tq, tk):
    i = pl.program_id(2)
    heads = []
    for hh in range(vt_scr.shape[0]):
        cols = pl.ds(hh * HEAD_DIM, HEAD_DIM)
        heads.append(functools.partial(
            _sb_head,q_ref.at[:, cols], k_ref.at[:, cols], v_ref.at[:, cols], sg_ref.at[:, cols],
            upper_ref, o_ref.at[:, cols], vt_scr.at[hh], acc_scr.at[hh], carry_scr.at[hh],
            i=i, tq=tq, tk=tk))

    @pl.when(i == 0)
    def _():
        for head in heads:
            head(first_step=True)

    @pl.when(i > 0)
    def _():
        for head in heads:
            head(first_step=False)


def _sb_head(q_ref, k_ref, v_ref, sg_ref, upper_ref, o_ref, vt_scr, acc_scr, carry_scr,
             *, i, tq, tk, first_step):
    if first_step:
        for c in range(vt_scr.shape[0]):
            vt_scr[c] = v_ref[c * tk:(c + 1) * tk, :].T

    q_t = q_ref[...].T
    upper = upper_ref[...]

    def scores(blk, half):
        start = pl.multiple_of(blk * tk, tk)
        return _dot(k_ref[pl.ds(start, tk), :], q_t[:, half * tk:(half + 1) * tk])

    def stage(z, strict):
        soft = jnp.log2(1.0 + jnp.exp2(-jnp.abs(z)))
        log_beta = jnp.minimum(z, 0.0) - soft
        log_keep = log_beta - z
        if strict is not None:
            log_keep = jnp.where(strict, log_keep, 0.0)
        later = _dot(upper, log_keep.astype(BF16))
        return log_beta + later, jnp.sum(log_keep, axis=0, keepdims=True)

    def group(chains, carry, acc):
        raw = [scores(blk, half) for blk, _, half, _ in chains[:SCORES_AHEAD]]
        staged = []
        for n, (_, strict, _, _) in enumerate(chains):
            if n + SCORES_AHEAD < len(chains):
                blk, _, half, _ = chains[n + SCORES_AHEAD]
                raw.append(scores(blk, half))
            staged.append(stage(raw[n], strict))
        carry, acc = list(carry), list(acc)
        for (blk, strict, half, valid), (pre, total) in zip(chains, staged):
            w = jnp.exp2(pre + carry[half])
            if strict is not None:
                w = jnp.where(strict, w, 0.0)
            if valid is not None:
                w = jnp.where(valid, w, 0.0)
                total = jnp.where(valid, total, 0.0)
            acc[half] = acc[half] + _dot(vt_scr[blk], w.astype(BF16))
            carry[half] = carry[half] + total
        acc_scr[...] = jnp.concatenate(acc, axis=1)
        carry_scr[...] = jnp.concatenate(carry, axis=1)

    key = lax.broadcasted_iota(jnp.int32, (tk, tk), 0)
    qry = lax.broadcasted_iota(jnp.int32, (tk, tk), 1)
    strict = key < qry
    first = 2 * i
    zeros = ([jnp.zeros((1, tk), F32)] * 2, [jnp.zeros((HEAD_DIM, tk), F32)] * 2)

    if first_step:
        group([(1, strict, 1, None), (0, strict, 0, None), (0, None, 1, None)], *zeros)
    else:
        group([(first + 1, strict, 1, None), (first, strict, 0, None),
               (first, None, 1, None), (first - 1, None, 0, None)], *zeros)

    def live(blk):
        return (blk >= 0) & (jnp.max(carry_scr[...]) > EXP2_UNDERFLOW)

    def body(blk):
        carry, acc = carry_scr[...], acc_scr[...]
        group([(blk, None, 1, None), (jnp.maximum(blk - 1, 0), None, 0, blk >= 1)],
              [carry[:, :tk], carry[:, tk:]], [acc[:, :tk], acc[:, tk:]])
        return blk - 1

    lax.while_loop(live, body, first - 1)
    o_ref[...] = (acc_scr[...].T * sg_ref[...].astype(F32)).astype(BF16)


def _sb_attention(qkvg, heads):
    b, s, _ = qkvg.shape
    tq = _tile(s, 512)
    tk = _tile(s, 256)
    assert tq == 2 * tk, "the kernel walks the query block as two halves of tk queries"
    per_step = min(SB_HEADS_PER_STEP, heads)
    groups, width = heads // per_step, per_step * HEAD_DIM
    upper =(lax.broadcasted_iota(jnp.int32, (tk, tk), 1)
             > lax.broadcasted_iota(jnp.int32, (tk, tk), 0)).astype(BF16)
    return pl.pallas_call(
        functools.partial(_sb_attn_kernel, tq=tq, tk=tk),
        out_shape=jax.ShapeDtypeStruct((b, s, heads * HEAD_DIM), BF16),
        grid=(b, groups, s // tq),
        in_specs=[
            pl.BlockSpec((None, tq, width), lambda bi, h, i: (bi, i, h)),
            pl.BlockSpec((None, s, width), lambda bi, h, i: (bi, 0, groups + h)),
            pl.BlockSpec((None, s, width), lambda bi, h, i: (bi, 0, 2 * groups + h)),
            pl.BlockSpec((None, tq, width), lambda bi, h, i: (bi, i, 3 * groups + h)),
            pl.BlockSpec((tk, tk), lambda bi, h, i: (0, 0)),
        ],
        out_specs=pl.BlockSpec((None, tq, width), lambda bi, h, i: (bi, i, h)),
        scratch_shapes=[pltpu.VMEM((per_step, s // tk, HEAD_DIM, tk), BF16),
                        pltpu.VMEM((per_step, HEAD_DIM, tq), F32),
                        pltpu.VMEM((per_step, 1, tq), F32)],
        compiler_params=_params("parallel", "parallel", "arbitrary"),
        name="sb_attn",
    )(qkvg, qkvg, qkvg, qkvg, upper)


def _out_kernel(og_ref, x_ref, p_ref, wout_ref, gpost_ref, wpp_ref, wpg_ref, gnext_ref,
                xo_ref, *next_refs):
    y = _dot(og_ref[...], wout_ref[...])
    x1 = x_ref[...] + y * _rms_scale(y) * gpost_ref[...]
    gate = _dot(x1.astype(BF16), wpg_ref[...])
    emb = _dot(p_ref[...].astype(BF16), wpp_ref[...])
    x2 = x1 + emb * _sigmoid(gate)
    xo_ref[...] = x2
    if next_refs:
        normed = x2 * _rms_scale(x2)
        for n, ref in enumerate(next_refs):
            ref[...] = (normed * gnext_ref[n:n + 1, :]).astype(BF16)


def _out_proj(og2d, x2d, p3d, layer, w_out, g_post, w_pp, w_pg, next_gains):
    t, d = x2d.shape
    width = og2d.shape[1]
    ple = p3d.shape[2]
    tm = _tile(t, 512)
    n_next = len(next_gains)
    gnext = jnp.stack(next_gains) if n_next else jnp.zeros((1, d), F32)
    row = lambda i: (i, 0)
    outs = pl.pallas_call(
        _out_kernel,
        out_shape=[jax.ShapeDtypeStruct((t, d), F32)]
        + [jax.ShapeDtypeStruct((t, d), BF16)] * n_next,
        grid=(t // tm,),
        in_specs=[
            pl.BlockSpec((tm, width), row),
            pl.BlockSpec((tm, d), row),
            pl.BlockSpec((None, tm, ple), lambda i: (layer, i, 0)),
            _resident((width, d)),
            _resident((1, d)),
            _resident((ple, d)),
            _resident((d, d)),
            _resident(gnext.shape),
        ],
        out_specs=[pl.BlockSpec((tm, d), row)] * (1 + n_next),
        compiler_params=_params("parallel"),
        name="out_proj",
    )(og2d, x2d, p3d, w_out, g_post.reshape(1, d), w_pp, w_pg, gnext)
    return outs[0], list(outs[1:])


def _kv_kernel(h_ref, wd_ref, gl_ref, wuk_ref, wuvt_ref, cos_ref, sin_ref, k_ref, vt_ref,
               *, rank, heads):
    c = _dot(h_ref[...], wd_ref[...])
    ckv = c[:, :rank]
    k_rope = c[:, rank:rank + ROPE_DIM]
    k_rot = c[:, rank + 128:rank + 128 + ROPE_DIM]
    latent = (ckv * _rms_scale(ckv) * gl_ref[...]).astype(BF16)
    k_nope = _dot(latent, wuk_ref[...]).astype(BF16)
    vt_ref[...] = _dot_nt(wuvt_ref[...], latent).astype(BF16)
    roped = (k_rope * cos_ref[...] + k_rot * sin_ref[...]).astype(BF16)
    for h in range(heads):
        k_ref[h, :, 0:HEAD_DIM] = k_nope[:, h * HEAD_DIM:(h + 1) * HEAD_DIM]
        k_ref[h, :, HEAD_DIM:QK_DIM] = roped


def _kv_side(hkv, wd, g_latent, w_uk, w_uv_t, cos, sin, heads):
    b, s, d = hkv.shape
    rank = w_uk.shape[0]
    tm = _tile(s, 512)
    row = lambda bi, i: (bi, i, 0)
    return pl.pallas_call(
        functools.partial(_kv_kernel, rank=rank, heads=heads),
        out_shape=[jax.ShapeDtypeStruct((b, heads, s, QK_DIM), BF16),
                   jax.ShapeDtypeStruct((b, heads * HEAD_DIM, s), BF16)],
        grid=(b, s // tm),
        in_specs=[
            pl.BlockSpec((None, tm, d), row),
            _resident(wd.shape),
            _resident((1, rank)),
            _resident(w_uk.shape),
            _resident(w_uv_t.shape),
            pl.BlockSpec((tm, ROPE_DIM), lambda bi, i: (i, 0)),
            pl.BlockSpec((tm, ROPE_DIM), lambda bi, i: (i, 0)),
        ],
        out_specs=[pl.BlockSpec((None, heads, tm, QK_DIM), lambda bi, i: (bi, 0, i, 0)),
                   pl.BlockSpec((None, heads * HEAD_DIM, tm), lambda bi, i: (bi, 0, i))],
        compiler_params=_params("parallel", "parallel"),
        name="kv_side",
    )(hkv, wd, g_latent.reshape(1, rank), w_uk, w_uv_t, cos, sin)


def _q_kernel(h_ref, win_ref, gq_ref, wnt_ref, wrt_ref, wrott_ref, cost_ref, sint_ref,
              qt_ref, sg_ref, *, rank, heads, scale):
    c = _dot(h_ref[...], win_ref[...])
    cq = c[:, :rank]
    gate = c[:, rank:]
    sg_ref[...] = (gate * _sigmoid(gate)).astype(BF16)
    latent = (cq * _rms_scale(cq) * gq_ref[...]).astype(BF16)
    nope_t = (_dot_nt(wnt_ref[...], latent) * scale).astype(BF16)
    cos_t = jnp.tile(cost_ref[...], (heads, 1))
    sin_t = jnp.tile(sint_ref[...], (heads, 1))
    roped_t = ((_dot_nt(wrt_ref[...], latent) * cos_t + _dot_nt(wrott_ref[...], latent) * sin_t)
               * scale).astype(BF16)
    for h in range(heads):
        qt_ref[h, 0:HEAD_DIM, :] = nope_t[h * HEAD_DIM:(h + 1) * HEAD_DIM, :]
        qt_ref[h, HEAD_DIM:QK_DIM, :] = roped_t[h * ROPE_DIM:(h + 1) * ROPE_DIM, :]


def _q_side(h, w_in, g_latent, w_n_t, w_r_t, w_rot_t, cos_t, sin_t, heads):
    b, s, d = h.shape
    rank = w_n_t.shape[1]
    width = w_in.shape[1] - rank
    tm = _tile(s, 512)
    row = lambda bi, i: (bi, i, 0)
    return pl.pallas_call(
        functools.partial(_q_kernel, rank=rank, heads=heads, scale=QK_DIM ** -0.5 * LOG2_E),
        out_shape=[jax.ShapeDtypeStruct((b, heads, QK_DIM, s), BF16),
                   jax.ShapeDtypeStruct((b, s, width), BF16)],
        grid=(b, s // tm),
        in_specs=[
            pl.BlockSpec((None, tm, d), row),
            _resident(w_in.shape),
            _resident((1, rank)),
            _resident(w_n_t.shape),
            _resident(w_r_t.shape),
            _resident(w_rot_t.shape),
            pl.BlockSpec((ROPE_DIM, tm), lambda bi, i: (0, i)),
            pl.BlockSpec((ROPE_DIM, tm), lambda bi, i: (0, i)),
        ],
        out_specs=[pl.BlockSpec((None, heads, QK_DIM, tm), lambda bi, i: (bi, 0, 0, i)),
                   pl.BlockSpec((None, tm, width), row)],
        compiler_params=_params("parallel", "parallel"),
        name="q_side",
    )(h, w_in, g_latent.reshape(1, rank), w_n_t, w_r_t, w_rot_t, cos_t, sin_t)


def _mla_attn_kernel(qt_ref, k_ref, vt_ref, sg_ref, o_ref, s_scr, p_scr, acc_scr, *, tq, tk):
    i = pl.program_id(2)
    heads = []
    for hh in range(HEADS_PER_STEP):
        cols = pl.ds(hh * HEAD_DIM, HEAD_DIM)
        heads.append(functools.partial(
            _mla_head, qt_ref.at[hh], k_ref.at[hh], vt_ref.at[cols, :], sg_ref.at[:, cols],
            o_ref.at[:, cols], s_scr.at[hh], p_scr.at[hh], acc_scr.at[hh], i=i, tq=tq, tk=tk))

    @pl.when(i == 0)
    def _():
        for head in heads:
            head(first_step=True)

    @pl.when(i > 0)
    def _():
        for head in heads:
            head(first_step=False)


def _mla_head(qt_ref, k_ref, vt_ref, sg_ref, o_ref, s_scr, p_scr, acc_scr,
              *, i, tq, tk, first_step):
    nblk = (i + 1) * (tq // tk)
    q_t = qt_ref[...]
    acc_scr[...] = jnp.zeros_like(acc_scr)

    def key_start(n):
        return pl.multiple_of(n * tk, tk)

    def step(n, parity, stats, run, *, merge, softmax, score, mask=None, steady=False):
        if merge and steady:
            out = _dot(vt_ref[:, pl.ds(key_start(n - 2), tk)], p_scr[parity])
        if score:
            s_scr[parity] = _dot(k_ref[pl.ds(key_start(n), tk), :], q_t)
        if merge and not steady:
            out = _dot(vt_ref[:, pl.ds(key_start(n - 2), tk)], p_scr[parity])
        m_c, l_c = stats
        if softmax:
            s = s_scr[1 - parity]
            if mask is not None:
                s = jnp.where(mask, s, NEG_INF)
            m_next = jnp.max(s, axis=0, keepdims=True)
            prob = jnp.exp2(s - m_next)
            p_scr[1 - parity] = prob.astype(BF16)
            stats = (m_next, jnp.sum(prob, axis=0, keepdims=True))
        if merge:
            m_run, l_run = run
            m_new = jnp.maximum(m_run, m_c)
            alpha = jnp.exp2(m_run - m_new)
            weight = jnp.exp2(m_c - m_new)
            acc_scr[...] = acc_scr[...] * alpha + out * weight
            run = (m_new, l_run * alpha + l_c * weight)
        return stats, run

    def finish(run):
        out = (acc_scr[...] / run[1]).T
        o_ref[...] = (out * sg_ref[...].astype(F32)).astype(BF16)

    kchunk = lax.broadcasted_iota(jnp.int32, (tk, tq), 0) // CHUNK
    qchunk = lax.broadcasted_iota(jnp.int32, (tk, tq), 1) // CHUNK
    first_diag = kchunk <= qchunk
    second_diag = (kchunk + tk // CHUNK) <= qchunk
    row = jnp.zeros((1, tq), F32)
    state = ((row, row), (jnp.full((1, tq), NEG_INF, F32), row))
    everything = dict(merge=True, softmax=True, score=True)

    if first_step:
        st = step(0, 0, *state, merge=False, softmax=False, score=True)
        st = step(1, 1, *st, merge=False, softmax=True, score=True, mask=first_diag)
        st = step(2, 0, *st, merge=True, softmax=True, score=False, mask=second_diag)
        st = step(3, 1, *st, merge=True, softmax=False, score=False)
        finish(st[1])
    else:
        st = step(0, 0, *state, merge=False, softmax=False, score=True)
        st = step(1, 1, *st, merge=False, softmax=True, score=True)
        st = step(2, 0, *st, **everything)

        def pair(t, st):
            st = step(3 + 2 * t, 1, *st, steady=True, **everything)
            return step(4 + 2 * t, 0, *st, steady=True, **everything)

        st = lax.fori_loop(0, (nblk - 4) // 2, pair, st)
        st = step(nblk - 1, 1, *st, mask=first_diag, **everything)
        st = step(nblk, 0, *st, merge=True, softmax=True, score=False, mask=second_diag)
        st = step(nblk + 1, 1, *st, merge=True, softmax=False, score=False)
        finish(st[1])


def _mla_attention(q_t, k_cat, v_t, sg):
    b, heads, _, s = q_t.shape
    tq = _tile(s, 512)
    tk = _tile(s, 256)
    assert tq == 2 * tk, "the pipeline below peels exactly two diagonal key blocks"
    width = HEADS_PER_STEP * HEAD_DIM
    return pl.pallas_call(
        functools.partial(_mla_attn_kernel, tq=tq, tk=tk),
        out_shape=jax.ShapeDtypeStruct((b, s, heads * HEAD_DIM), BF16),
        grid=(b, heads // HEADS_PER_STEP, s // tq),
        in_specs=[
            pl.BlockSpec((None, HEADS_PER_STEP, QK_DIM, tq), lambda bi, h, i: (bi, h, 0, i)),
            pl.BlockSpec((None, HEADS_PER_STEP, s, QK_DIM), lambda bi, h, i: (bi, h, 0, 0)),
            pl.BlockSpec((None, width, s), lambda bi, h, i: (bi, h, 0)),
            pl.BlockSpec((None, tq, width), lambda bi, h, i: (bi, i, h)),
        ],
        out_specs=pl.BlockSpec((None, tq, width), lambda bi, h, i: (bi, i, h)),
        scratch_shapes=[pltpu.VMEM((HEADS_PER_STEP, 2, tk, tq), F32),
                        pltpu.VMEM((HEADS_PER_STEP, 2, tk, tq), BF16),
                        pltpu.VMEM((HEADS_PER_STEP, HEAD_DIM, tq), F32)],
        compiler_params=_params("parallel", "parallel", "arbitrary"),
        name="mla_attn",
    )(q_t, k_cat, v_t, sg)


def _rotate_half_columns(w):
    half = ROPE_DIM // 2
    return jnp.concatenate([-w[..., half:], w[..., :half]], axis=-1)


def _rope_tables(s):
    inv = 1.0 / (ROPE_THETA ** (jnp.arange(0, ROPE_DIM, 2, dtype=F32) / ROPE_DIM))
    ang = jnp.arange(s, dtype=F32)[:, None] * inv[None, :]
    cos, sin = jnp.cos(ang), jnp.sin(ang)
    return jnp.tile(cos, (1, 2)), jnp.tile(sin, (1, 2))


def kernel(x, p, norm_pre, norm_post, w_in_a, w_out_a, w_in_b, q_latent_norm, w_uq, w_out_b,
           kv_norm, w_dkv, kv_latent_norm, w_uk, w_uv, w_ple_proj, w_ple_gate):
    b, s, d = x.shape
    t = b * s
    n_a, n_b = w_in_a.shape[0], w_in_b.shape[0]
    heads = d // HEAD_DIM
    q_rank = w_uq.shape[1]
    kv_rank = w_uk.shape[0]
    cos, sin = _rope_tables(s)

    x2d = x.reshape(t, d)
    p3d = p.reshape(p.shape[0], t, p.shape[-1])

    def finish_layer(layer, og, x2d, w_out):
        if layer + 1 < n_a:
            gains = [norm_pre[layer + 1]]
        elif layer + 1 == n_a and n_b:
            gains = [norm_pre[layer + 1], kv_norm]
        elif layer + 1 < n_a + n_b:
            gains = [norm_pre[layer + 1]]
        else:
            gains = []
        return _out_proj(og.reshape(t, -1), x2d, p3d, layer, w_out.astype(BF16),
                         norm_post[layer], w_ple_proj[layer].astype(BF16),
                         w_ple_gate[layer].astype(BF16), gains)

    h_next = None
    for i in range(n_a):
        qkvg = _proj_a(x2d, norm_pre[i], w_in_a[i])
        og = _sb_attention(qkvg.reshape(b, s, -1), heads)
        x2d, h_next = finish_layer(i, og, x2d, w_out_a[i])

    if n_b == 0:
        return x2d.reshape(b, s, d)
    if n_a == 0:
        raise NotImplementedError("a trunk without mixer-A layers is not supported")

    h_b, h_kv = h_next
    wd = jnp.concatenate([
        w_dkv[:, :kv_rank],
        jnp.pad(w_dkv[:, kv_rank:], ((0, 0), (0, 128 - ROPE_DIM))),
        jnp.pad(_rotate_half_columns(w_dkv[:, kv_rank:]), ((0, 0), (0, 128 - ROPE_DIM))),
    ], axis=1).astype(BF16)
    k_cat, v_t = _kv_side(h_kv.reshape(b, s, d), wd, kv_latent_norm, w_uk.astype(BF16),
                          w_uv.T.astype(BF16), cos, sin, heads)

    for j in range(n_b):
        layer = n_a + j
        wq = w_uq[j].reshape(q_rank, heads, QK_DIM)
        w_n_t = wq[:, :, :HEAD_DIM].reshape(q_rank, heads * HEAD_DIM).T.astype(BF16)
        w_r_t = wq[:, :, HEAD_DIM:].reshape(q_rank, heads * ROPE_DIM).T.astype(BF16)
        w_rot_t = _rotate_half_columns(wq[:, :, HEAD_DIM:]).reshape(
            q_rank, heads * ROPE_DIM).T.astype(BF16)
        q_t, sg = _q_side(h_b.reshape(b, s, d), w_in_b[j].astype(BF16), q_latent_norm[j],
                          w_n_t, w_r_t, w_rot_t, cos.T, sin.T, heads)
        og = _mla_attention(q_t, k_cat, v_t, sg)
        x2d, h_next = finish_layer(layer, og, x2d, w_out_b[j])
        if h_next:
            h_b = h_next[0]
    return x2d.reshape(b, s, d)
```

```python
import functools
import math

import jax
import jax.numpy as jnp
from jax import lax
from jax.experimental import pallas as pl
from jax.experimental.pallas import tpu as pltpu

RMS_EPS = 1e-6
NEG_INF = -1e30
HEAD_DIM = 128
ROPE_DIM = 64
QK_DIM = HEAD_DIM + ROPE_DIM
CHUNK = 64
HEADS_PER_STEP = 4
SCORES_AHEAD = 2
ROPE_THETA = 10000.0
LOG2_E = math.log2(math.e)
EXP2_UNDERFLOW = -160.0

V7X_VMEM_LIMIT_BYTES = 56 * 1024 * 1024
BF16 = jnp.bfloat16
F32 = jnp.float32


def _tile(n, pref):
    t = min(n, pref)
    while n % t:
        t //= 2
    return t


def _params(*sem):
    return pltpu.CompilerParams(dimension_semantics=sem,
                                vmem_limit_bytes=V7X_VMEM_LIMIT_BYTES)


def _resident(shape):
    nd = len(shape)
    return pl.BlockSpec(shape, lambda *_: (0,) * nd, pipeline_mode=pl.Buffered(1))


def _rms_scale(v):
    return lax.rsqrt(jnp.mean(v * v, axis=-1, keepdims=True) + RMS_EPS)


def _sigmoid(v):
    return 1.0 / (1.0 + jnp.exp(-v))


def _dot(a, b):
    return jnp.dot(a, b, preferred_element_type=F32)


def _dot_nt(a, b):
    return lax.dot_general(a, b, (((1,), (1,)), ((), ())), preferred_element_type=F32)


def _proj_a_kernel(x_ref, g_ref, w_ref, o_ref, h_scr, *, q_blocks, q_scale):
    j = pl.program_id(1)

    @pl.when(j == 0)
    def _():
        x = x_ref[...]
        h_scr[...] = (x * _rms_scale(x) * g_ref[...]).astype(BF16)

    acc = _dot(h_scr[...], w_ref[...].astype(BF16))

    @pl.when(j < q_blocks)
    def _():
        o_ref[...] = (acc * q_scale).astype(BF16)

    @pl.when((j >= q_blocks) & (j < 3 * q_blocks))
    def _():
        o_ref[...] = acc.astype(BF16)

    @pl.when(j >= 3 * q_blocks)
    def _():
        o_ref[...] = (acc * _sigmoid(acc)).astype(BF16)


def _proj_a(x2d, gain, w_in):
    t, d = x2d.shape
    n = w_in.shape[1]
    width = n // 4
    tm = _tile(t, 1024)
    tn = _tile(width, 1024)
    return pl.pallas_call(
        functools.partial(_proj_a_kernel, q_blocks=width // tn,
                          q_scale=HEAD_DIM ** -0.5 * LOG2_E),
        out_shape=jax.ShapeDtypeStruct((t, n), BF16),
        grid=(t // tm, n // tn),
        in_specs=[
            pl.BlockSpec((tm, d), lambda i, j: (i, 0)),
            pl.BlockSpec((1, d), lambda i, j: (0, 0)),
            pl.BlockSpec((d, tn), lambda i, j: (0, j)),
        ],
        out_specs=pl.BlockSpec((tm, tn), lambda i, j: (i, j)),
        scratch_shapes=[pltpu.VMEM((tm, d), BF16)],
        compiler_params=_params("parallel", "arbitrary"),
        name="proj_a",
    )(x2d, gain.reshape(1, d), w_in)


def _sb_attn_kernel(q_ref, k_ref, v_ref, sg_ref, upper_ref, o_ref, vt_scr, acc_scr, carry_scr,
                    *, tq, tk):
    i = pl.program_id(2)
    heads = []
    for hh in range(HEADS_PER_STEP):
        cols = pl.ds(hh * HEAD_DIM, HEAD_DIM)
        heads.append(functools.partial(
            _sb_head, q_ref.at[:, cols], k_ref.at[:, cols], v_ref.at[:, cols], sg_ref.at[:, cols],
            upper_ref, o_ref.at[:, cols], vt_scr.at[hh], acc_scr.at[hh], carry_scr.at[hh],
            i=i, tq=tq, tk=tk))

    @pl.when(i == 0)
    def _():
        for head in heads:
            head(phase="first")

    @pl.when(i > 0)
    def _():
        for head in heads:
            head(phase="later")

    def live(blk):
        return (blk >= 0) & (jnp.max(carry_scr[...]) > EXP2_UNDERFLOW)

    def body(blk):
        for head in heads:
            head(phase="advance", blk=blk)
        return blk - 1

    lax.while_loop(live, body, 2 * i - 1)
    for head in heads:
        head(phase="finish")


def _sb_head(q_ref, k_ref, v_ref, sg_ref, upper_ref, o_ref, vt_scr, acc_scr, carry_scr,
             *, i, tq, tk, phase, blk=None):
    if phase == "finish":
        o_ref[...] = (acc_scr[...].T * sg_ref[...].astype(F32)).astype(BF16)
        return
    if phase == "first":
        for c in range(vt_scr.shape[0]):
            vt_scr[c] = v_ref[c * tk:(c + 1) * tk, :].T

    q_t = q_ref[...].T
    upper = upper_ref[...]

    def scores(blk, half):
        start = pl.multiple_of(blk * tk, tk)
        return _dot(k_ref[pl.ds(start, tk), :], q_t[:, half * tk:(half + 1) * tk])

    def stage(z, strict):
        soft = jnp.log2(1.0 + jnp.exp2(-jnp.abs(z)))
        log_beta = jnp.minimum(z, 0.0) - soft
        log_keep = log_beta - z
        if strict is not None:
            log_keep = jnp.where(strict, log_keep, 0.0)
        later = _dot(upper, log_keep.astype(BF16))
        return log_beta + later, jnp.sum(log_keep, axis=0, keepdims=True)

    def group(chains, carry, acc):
        raw = [scores(blk, half) for blk, _, half, _ in chains[:SCORES_AHEAD]]
        staged = []
        for n, (_, strict, _, _) in enumerate(chains):
            if n + SCORES_AHEAD < len(chains):
                blk, _, half, _ = chains[n + SCORES_AHEAD]
                raw.append(scores(blk, half))
            staged.append(stage(raw[n], strict))
        carry, acc = list(carry), list(acc)
        for (blk, strict, half, valid), (pre, total) in zip(chains, staged):
            w = jnp.exp2(pre + carry[half])
            if strict is not None:
                w = jnp.where(strict, w, 0.0)
            if valid is not None:
                w = jnp.where(valid, w, 0.0)
                total = jnp.where(valid, total, 0.0)
            acc[half] = acc[half] + _dot(vt_scr[blk], w.astype(BF16))
            carry[half] = carry[half] + total
        acc_scr[...] = jnp.concatenate(acc, axis=1)
        carry_scr[...] = jnp.concatenate(carry, axis=1)

    key = lax.broadcasted_iota(jnp.int32, (tk, tk), 0)
    qry = lax.broadcasted_iota(jnp.int32, (tk, tk), 1)
    strict = key < qry
    first = 2 * i
    zeros = ([jnp.zeros((1, tk), F32)] * 2, [jnp.zeros((HEAD_DIM, tk), F32)] * 2)

    if phase == "first":
        group([(1, strict, 1, None), (0, strict, 0, None), (0, None, 1, None)], *zeros)
    elif phase == "later":
        group([(first + 1, strict, 1, None), (first, strict, 0, None),
               (first, None, 1, None), (first - 1, None, 0, None)], *zeros)
    else:
        carry, acc = carry_scr[...], acc_scr[...]
        group([(blk, None, 1, None), (jnp.maximum(blk - 1, 0), None, 0, blk >= 1)],
              [carry[:, :tk], carry[:, tk:]], [acc[:, :tk], acc[:, tk:]])


def _sb_attention(qkvg, heads):
    b, s, _ = qkvg.shape
    tq = _tile(s, 512)
    tk = _tile(s, 256)
    assert tq == 2 * tk, "the kernel walks the query block as two halves of tk queries"
    groups, width = heads // HEADS_PER_STEP, HEADS_PER_STEP * HEAD_DIM
    upper = (lax.broadcasted_iota(jnp.int32, (tk, tk), 1)
             > lax.broadcasted_iota(jnp.int32, (tk, tk), 0)).astype(BF16)
    return pl.pallas_call(
        functools.partial(_sb_attn_kernel, tq=tq, tk=tk),
        out_shape=jax.ShapeDtypeStruct((b, s, heads * HEAD_DIM), BF16),
        grid=(b, groups, s // tq),
        in_specs=[
            pl.BlockSpec((None, tq, width), lambda bi, h, i: (bi, i, h)),
            pl.BlockSpec((None, s, width), lambda bi, h, i: (bi, 0, groups + h)),
            pl.BlockSpec((None, s, width), lambda bi, h, i: (bi, 0, 2 * groups + h)),
            pl.BlockSpec((None, tq, width), lambda bi, h, i: (bi, i, 3 * groups + h)),
            pl.BlockSpec((tk, tk), lambda bi, h, i: (0, 0)),
        ],
        out_specs=pl.BlockSpec((None, tq, width), lambda bi, h, i: (bi, i, h)),
        scratch_shapes=[pltpu.VMEM((HEADS_PER_STEP, s // tk, HEAD_DIM, tk), BF16),
                        pltpu.VMEM((HEADS_PER_STEP, HEAD_DIM, tq), F32),
                        pltpu.VMEM((HEADS_PER_STEP, 1, tq), F32)],
        compiler_params=_params("parallel", "parallel", "arbitrary"),
        name="sb_attn",
    )(qkvg, qkvg, qkvg, qkvg, upper)


def _out_kernel(og_ref, x_ref, p_ref, wout_ref, gpost_ref, wpp_ref, wpg_ref, gnext_ref,
                xo_ref, *next_refs):
    y = _dot(og_ref[...], wout_ref[...])
    x1 = x_ref[...] + y * _rms_scale(y) * gpost_ref[...]
    gate = _dot(x1.astype(BF16), wpg_ref[...])
    emb = _dot(p_ref[...].astype(BF16), wpp_ref[...])
    x2 = x1 + emb * _sigmoid(gate)
    xo_ref[...] = x2
    if next_refs:
        normed = x2 * _rms_scale(x2)
        for n, ref in enumerate(next_refs):
            ref[...] = (normed * gnext_ref[n:n + 1, :]).astype(BF16)


def _out_proj(og2d, x2d, p3d, layer, w_out, g_post, w_pp, w_pg, next_gains):
    t, d = x2d.shape
    width = og2d.shape[1]
    ple = p3d.shape[2]
    tm = _tile(t, 512)
    n_next = len(next_gains)
    gnext = jnp.stack(next_gains) if n_next else jnp.zeros((1, d), F32)
    row = lambda i: (i, 0)
    outs = pl.pallas_call(
        _out_kernel,
        out_shape=[jax.ShapeDtypeStruct((t, d), F32)]
        + [jax.ShapeDtypeStruct((t, d), BF16)] * n_next,
        grid=(t // tm,),
        in_specs=[
            pl.BlockSpec((tm, width), row),
            pl.BlockSpec((tm, d), row),
            pl.BlockSpec((None, tm, ple), lambda i: (layer, i, 0)),
            _resident((width, d)),
            _resident((1, d)),
            _resident((ple, d)),
            _resident((d, d)),
            _resident(gnext.shape),
        ],
        out_specs=[pl.BlockSpec((tm, d), row)] * (1 + n_next),
        compiler_params=_params("parallel"),
        name="out_proj",
    )(og2d, x2d, p3d, w_out, g_post.reshape(1, d), w_pp, w_pg, gnext)
    return outs[0], list(outs[1:])


def _kv_kernel(h_ref, wd_ref, gl_ref, wuk_ref, wuvt_ref, cos_ref, sin_ref, k_ref, vt_ref,
               *, rank, heads):
    c = _dot(h_ref[...], wd_ref[...])
    ckv = c[:, :rank]
    k_rope = c[:, rank:rank + ROPE_DIM]
    k_rot = c[:, rank + 128:rank + 128 + ROPE_DIM]
    latent = (ckv * _rms_scale(ckv) * gl_ref[...]).astype(BF16)
    k_nope = _dot(latent, wuk_ref[...]).astype(BF16)
    vt_ref[...] = _dot_nt(wuvt_ref[...], latent).astype(BF16)
    roped = (k_rope * cos_ref[...] + k_rot * sin_ref[...]).astype(BF16)
    for h in range(heads):
        k_ref[h, :, 0:HEAD_DIM] = k_nope[:, h * HEAD_DIM:(h + 1) * HEAD_DIM]
        k_ref[h, :, HEAD_DIM:QK_DIM] = roped


def _kv_side(hkv, wd, g_latent, w_uk, w_uv_t, cos, sin, heads):
    b, s, d = hkv.shape
    rank = w_uk.shape[0]
    tm = _tile(s, 512)
    row = lambda bi, i: (bi, i, 0)
    return pl.pallas_call(
        functools.partial(_kv_kernel, rank=rank, heads=heads),
        out_shape=[jax.ShapeDtypeStruct((b, heads, s, QK_DIM), BF16),
                   jax.ShapeDtypeStruct((b, heads * HEAD_DIM, s), BF16)],
        grid=(b, s // tm),
        in_specs=[
            pl.BlockSpec((None, tm, d), row),
            _resident(wd.shape),
            _resident((1, rank)),
            _resident(w_uk.shape),
            _resident(w_uv_t.shape),
            pl.BlockSpec((tm, ROPE_DIM), lambda bi, i: (i, 0)),
            pl.BlockSpec((tm, ROPE_DIM), lambda bi, i: (i, 0)),
        ],
        out_specs=[pl.BlockSpec((None, heads, tm, QK_DIM), lambda bi, i: (bi, 0, i, 0)),
                   pl.BlockSpec((None, heads * HEAD_DIM, tm), lambda bi, i: (bi, 0, i))],
        compiler_params=_params("parallel", "parallel"),
        name="kv_side",
    )(hkv, wd, g_latent.reshape(1, rank), w_uk, w_uv_t, cos, sin)


def _q_kernel(h_ref, win_ref, gq_ref, wnt_ref, wrt_ref, wrott_ref, cost_ref, sint_ref,
              qt_ref, sg_ref, *, rank, heads, scale):
    c = _dot(h_ref[...], win_ref[...])
    cq = c[:, :rank]
    gate = c[:, rank:]
    sg_ref[...] = (gate * _sigmoid(gate)).astype(BF16)
    latent = (cq * _rms_scale(cq) * gq_ref[...]).astype(BF16)
    nope_t = (_dot_nt(wnt_ref[...], latent) * scale).astype(BF16)
    cos_t = jnp.tile(cost_ref[...], (heads, 1))
    sin_t = jnp.tile(sint_ref[...], (heads, 1))
    roped_t = ((_dot_nt(wrt_ref[...], latent) * cos_t + _dot_nt(wrott_ref[...], latent) * sin_t)
               * scale).astype(BF16)
    for h in range(heads):
        qt_ref[h, 0:HEAD_DIM, :] = nope_t[h * HEAD_DIM:(h + 1) * HEAD_DIM, :]
        qt_ref[h, HEAD_DIM:QK_DIM, :] = roped_t[h * ROPE_DIM:(h + 1) * ROPE_DIM, :]


def _q_side(h, w_in, g_latent, w_n_t, w_r_t, w_rot_t, cos_t, sin_t, heads):
    b, s, d = h.shape
    rank = w_n_t.shape[1]
    width = w_in.shape[1] - rank
    tm = _tile(s, 512)
    row = lambda bi, i: (bi, i, 0)
    return pl.pallas_call(
        functools.partial(_q_kernel, rank=rank, heads=heads, scale=QK_DIM ** -0.5 * LOG2_E),
        out_shape=[jax.ShapeDtypeStruct((b, heads, QK_DIM, s), BF16),
                   jax.ShapeDtypeStruct((b, s, width), BF16)],
        grid=(b, s // tm),
        in_specs=[
            pl.BlockSpec((None, tm, d), row),
            _resident(w_in.shape),
            _resident((1, rank)),
            _resident(w_n_t.shape),
            _resident(w_r_t.shape),
            _resident(w_rot_t.shape),
            pl.BlockSpec((ROPE_DIM, tm), lambda bi, i: (0, i)),
            pl.BlockSpec((ROPE_DIM, tm), lambda bi, i: (0, i)),
        ],
        out_specs=[pl.BlockSpec((None, heads, QK_DIM, tm), lambda bi, i: (bi, 0, 0, i)),
                   pl.BlockSpec((None, tm, width), row)],
        compiler_params=_params("parallel", "parallel"),
        name="q_side",
    )(h, w_in, g_latent.reshape(1, rank), w_n_t, w_r_t, w_rot_t, cos_t, sin_t)


def _mla_attn_kernel(qt_ref, k_ref, vt_ref, sg_ref, o_ref, s_scr, p_scr, acc_scr, *, tq, tk):
    i = pl.program_id(2)
    heads = []
    for hh in range(HEADS_PER_STEP):
        cols = pl.ds(hh * HEAD_DIM, HEAD_DIM)
        heads.append(functools.partial(
            _mla_head, qt_ref.at[hh], k_ref.at[hh], vt_ref.at[cols, :], sg_ref.at[:, cols],
            o_ref.at[:, cols], s_scr.at[hh], p_scr.at[hh], acc_scr.at[hh], i=i, tq=tq, tk=tk))

    @pl.when(i == 0)
    def _():
        for head in heads:
            head(first_step=True)

    @pl.when(i > 0)
    def _():
        for head in heads:
            head(first_step=False)


def _mla_head(qt_ref, k_ref, vt_ref, sg_ref, o_ref, s_scr, p_scr, acc_scr,
              *, i, tq, tk, first_step):
    nblk = (i + 1) * (tq // tk)
    q_t = qt_ref[...]
    acc_scr[...] = jnp.zeros_like(acc_scr)

    def key_start(n):
        return pl.multiple_of(n * tk, tk)

    def step(n, parity, stats, run, *, merge, softmax, score, mask=None, steady=False):
        if merge and steady:
            out = _dot(vt_ref[:, pl.ds(key_start(n - 2), tk)], p_scr[parity])
        if score:
            s_scr[parity] = _dot(k_ref[pl.ds(key_start(n), tk), :], q_t)
        if merge and not steady:
            out = _dot(vt_ref[:, pl.ds(key_start(n - 2), tk)], p_scr[parity])
        m_c, l_c = stats
        if softmax:
            s = s_scr[1 - parity]
            if mask is not None:
                s = jnp.where(mask, s, NEG_INF)
            m_next = jnp.max(s, axis=0, keepdims=True)
            prob = jnp.exp2(s - m_next)
            p_scr[1 - parity] = prob.astype(BF16)
            stats = (m_next, jnp.sum(prob, axis=0, keepdims=True))
        if merge:
            m_run, l_run = run
            m_new = jnp.maximum(m_run, m_c)
            alpha = jnp.exp2(m_run - m_new)
            weight = jnp.exp2(m_c - m_new)
            acc_scr[...] = acc_scr[...] * alpha + out * weight
            run = (m_new, l_run * alpha + l_c * weight)
        return stats, run

    def finish(run):
        out = (acc_scr[...] / run[1]).T
        o_ref[...] = (out * sg_ref[...].astype(F32)).astype(BF16)

    kchunk = lax.broadcasted_iota(jnp.int32, (tk, tq), 0) // CHUNK
    qchunk = lax.broadcasted_iota(jnp.int32, (tk, tq), 1) // CHUNK
    first_diag = kchunk <= qchunk
    second_diag = (kchunk + tk // CHUNK) <= qchunk
    row = jnp.zeros((1, tq), F32)
    state = ((row, row), (jnp.full((1, tq), NEG_INF, F32), row))
    everything = dict(merge=True, softmax=True, score=True)

    if first_step:
        st = step(0, 0, *state, merge=False, softmax=False, score=True)
        st = step(1, 1, *st, merge=False, softmax=True, score=True, mask=first_diag)
        st = step(2, 0, *st, merge=True, softmax=True, score=False, mask=second_diag)
        st = step(3, 1, *st, merge=True, softmax=False, score=False)
        finish(st[1])
    else:
        st = step(0, 0, *state, merge=False, softmax=False, score=True)
        st = step(1, 1, *st, merge=False, softmax=True, score=True)
        st = step(2, 0, *st, **everything)

        def pair(t, st):
            st = step(3 + 2 * t, 1, *st, steady=True, **everything)
            return step(4 + 2 * t, 0, *st, steady=True, **everything)

        st = lax.fori_loop(0, (nblk - 4) // 2, pair, st)
        st = step(nblk - 1, 1, *st, mask=first_diag, **everything)
        st = step(nblk, 0, *st, merge=True, softmax=True, score=False, mask=second_diag)
        st = step(nblk + 1, 1, *st, merge=True, softmax=False, score=False)
        finish(st[1])


def _mla_attention(q_t, k_cat, v_t, sg):
    b, heads, _, s = q_t.shape
    tq = _tile(s, 512)
    tk = _tile(s, 256)
    assert tq == 2 * tk, "the pipeline below peels exactly two diagonal key blocks"
    width = HEADS_PER_STEP * HEAD_DIM
    return pl.pallas_call(
        functools.partial(_mla_attn_kernel, tq=tq, tk=tk),
        out_shape=jax.ShapeDtypeStruct((b, s, heads * HEAD_DIM), BF16),
        grid=(b, heads // HEADS_PER_STEP, s // tq),
        in_specs=[
            pl.BlockSpec((None, HEADS_PER_STEP, QK_DIM, tq), lambda bi, h, i: (bi, h, 0, i)),
            pl.BlockSpec((None, HEADS_PER_STEP, s, QK_DIM), lambda bi, h, i: (bi, h, 0, 0)),
            pl.BlockSpec((None, width, s), lambda bi, h, i: (bi, h, 0)),
            pl.BlockSpec((None, tq, width), lambda bi, h, i: (bi, i, h)),
        ],
        out_specs=pl.BlockSpec((None, tq, width), lambda bi, h, i: (bi, i, h)),
        scratch_shapes=[pltpu.VMEM((HEADS_PER_STEP, 2, tk, tq), F32),
                        pltpu.VMEM((HEADS_PER_STEP, 2, tk, tq), BF16),
                        pltpu.VMEM((HEADS_PER_STEP, HEAD_DIM, tq), F32)],
        compiler_params=_params("parallel", "parallel", "arbitrary"),
        name="mla_attn",
    )(q_t, k_cat, v_t, sg)


def _rotate_half_columns(w):
    half = ROPE_DIM // 2
    return jnp.concatenate([-w[..., half:], w[..., :half]], axis=-1)


def _rope_tables(s):
    inv = 1.0 / (ROPE_THETA ** (jnp.arange(0, ROPE_DIM, 2, dtype=F32) / ROPE_DIM))
    ang = jnp.arange(s, dtype=F32)[:, None] * inv[None, :]
    cos, sin = jnp.cos(ang), jnp.sin(ang)
    return jnp.tile(cos, (1, 2)), jnp.tile(sin, (1, 2))


def kernel(x, p, norm_pre, norm_post, w_in_a, w_out_a, w_in_b, q_latent_norm, w_uq, w_out_b,
           kv_norm, w_dkv, kv_latent_norm, w_uk, w_uv, w_ple_proj, w_ple_gate):
    b, s, d = x.shape
    t = b * s
    n_a, n_b = w_in_a.shape[0], w_in_b.shape[0]
    heads = d // HEAD_DIM
    q_rank = w_uq.shape[1]
    kv_rank = w_uk.shape[0]
    cos, sin = _rope_tables(s)

    x2d = x.reshape(t, d)
    p3d = p.reshape(p.shape[0], t, p.shape[-1])

    def finish_layer(layer, og, x2d, w_out):
        if layer + 1 < n_a:
            gains = [norm_pre[layer + 1]]
        elif layer + 1 == n_a and n_b:
            gains = [norm_pre[layer + 1], kv_norm]
        elif layer + 1 < n_a + n_b:
            gains = [norm_pre[layer + 1]]
        else:
            gains = []
        return _out_proj(og.reshape(t, -1), x2d, p3d, layer, w_out.astype(BF16),
                         norm_post[layer], w_ple_proj[layer].astype(BF16),
                         w_ple_gate[layer].astype(BF16), gains)

    h_next = None
    for i in range(n_a):
        qkvg = _proj_a(x2d, norm_pre[i], w_in_a[i])
        og = _sb_attention(qkvg.reshape(b, s, -1), heads)
        x2d, h_next = finish_layer(i, og, x2d, w_out_a[i])

    if n_b == 0:
        return x2d.reshape(b, s, d)
    if n_a == 0:
        raise NotImplementedError("a trunk without mixer-A layers is not supported")

    h_b, h_kv = h_next
    wd = jnp.concatenate([
        w_dkv[:, :kv_rank],
        jnp.pad(w_dkv[:, kv_rank:], ((0, 0), (0, 128 - ROPE_DIM))),
        jnp.pad(_rotate_half_columns(w_dkv[:, kv_rank:]), ((0, 0), (0, 128 - ROPE_DIM))),
    ], axis=1).astype(BF16)
    k_cat, v_t = _kv_side(h_kv.reshape(b, s, d), wd, kv_latent_norm, w_uk.astype(BF16),
                          w_uv.T.astype(BF16), cos, sin, heads)

    for j in range(n_b):
        layer = n_a + j
        wq = w_uq[j].reshape(q_rank, heads, QK_DIM)
        w_n_t = wq[:, :, :HEAD_DIM].reshape(q_rank, heads * HEAD_DIM).T.astype(BF16)
        w_r_t = wq[:, :, HEAD_DIM:].reshape(q_rank, heads * ROPE_DIM).T.astype(BF16)
        w_rot_t = _rotate_half_columns(wq[:, :, HEAD_DIM:]).reshape(
            q_rank, heads * ROPE_DIM).T.astype(BF16)
        q_t, sg = _q_side(h_b.reshape(b, s, d), w_in_b[j].astype(BF16), q_latent_norm[j],
                          w_n_t, w_r_t, w_rot_t, cos.T, sin.T, heads)
        og = _mla_attention(q_t, k_cat, v_t, sg)
        x2d, h_next = finish_layer(layer, og, x2d, w_out_b[j])
        if h_next:
            h_b = h_next[0]
    return x2d.reshape(b, s, d)
```

```python
import functools
import math

import jax
import jax.numpy as jnp
from jax import lax
from jax.experimental import pallas as pl
from jax.experimental.pallas import tpu as pltpu

RMS_EPS = 1e-6
NEG_INF = -1e30
HEAD_DIM = 128
ROPE_DIM = 64
QK_DIM = HEAD_DIM + ROPE_DIM
CHUNK = 64
HEADS_PER_STEP = 4
SCORES_AHEAD = 2
ROPE_THETA = 10000.0
LOG2_E = math.log2(math.e)
EXP2_UNDERFLOW = -160.0

V7X_VMEM_LIMIT_BYTES = 56 * 1024 * 1024
BF16 = jnp.bfloat16
F32 = jnp.float32


def _tile(n, pref):
    t = min(n, pref)
    while n % t:
        t //= 2
    return t


def _params(*sem):
    return pltpu.CompilerParams(dimension_semantics=sem,
                                vmem_limit_bytes=V7X_VMEM_LIMIT_BYTES)


def _resident(shape):
    nd = len(shape)
    return pl.BlockSpec(shape, lambda *_: (0,) * nd, pipeline_mode=pl.Buffered(1))


def _rms_scale(v):
    return lax.rsqrt(jnp.mean(v * v, axis=-1, keepdims=True) + RMS_EPS)


def _sigmoid(v):
    return 1.0 / (1.0 + jnp.exp(-v))


def _dot(a, b):
    return jnp.dot(a, b, preferred_element_type=F32)


def _dot_nt(a, b):
    return lax.dot_general(a, b, (((1,), (1,)), ((), ())), preferred_element_type=F32)


def _proj_a_kernel(x_ref, g_ref, w_ref, o_ref, h_scr, *, q_blocks, q_scale):
    j = pl.program_id(1)

    @pl.when(j == 0)
    def _():
        x = x_ref[...]
        h_scr[...] = (x * _rms_scale(x) * g_ref[...]).astype(BF16)

    acc = _dot(h_scr[...], w_ref[...].astype(BF16))

    @pl.when(j < q_blocks)
    def _():
        o_ref[...] = (acc * q_scale).astype(BF16)

    @pl.when((j >= q_blocks) & (j < 3 * q_blocks))
    def _():
        o_ref[...] = acc.astype(BF16)

    @pl.when(j >= 3 * q_blocks)
    def _():
        o_ref[...] = (acc * _sigmoid(acc)).astype(BF16)


def _proj_a(x2d, gain, w_in):
    t, d = x2d.shape
    n = w_in.shape[1]
    width = n // 4
    tm = _tile(t, 1024)
    tn = _tile(width, 1024)
    return pl.pallas_call(
        functools.partial(_proj_a_kernel, q_blocks=width // tn,
                          q_scale=HEAD_DIM ** -0.5 * LOG2_E),
        out_shape=jax.ShapeDtypeStruct((t, n), BF16),
        grid=(t // tm, n // tn),
        in_specs=[
            pl.BlockSpec((tm, d), lambda i, j: (i, 0)),
            pl.BlockSpec((1, d), lambda i, j: (0, 0)),
            pl.BlockSpec((d, tn), lambda i, j: (0, j)),
        ],
        out_specs=pl.BlockSpec((tm, tn), lambda i, j: (i, j)),
        scratch_shapes=[pltpu.VMEM((tm, d), BF16)],
        compiler_params=_params("parallel", "arbitrary"),
        name="proj_a",
    )(x2d, gain.reshape(1, d), w_in)


def _sb_attn_kernel(q_ref, k_ref, v_ref, sg_ref, upper_ref, o_ref, vt_scr, acc_scr, carry_scr,
                    *, tq, tk):
    i = pl.program_id(2)
    heads = []
    for hh in range(HEADS_PER_STEP):
        cols = pl.ds(hh * HEAD_DIM, HEAD_DIM)
        heads.append(functools.partial(
            _sb_head, q_ref.at[:, cols], k_ref.at[:, cols], v_ref.at[:, cols], sg_ref.at[:, cols],
            upper_ref, o_ref.at[:, cols], vt_scr.at[hh], acc_scr.at[hh], carry_scr.at[hh],
            i=i, tq=tq, tk=tk))

    @pl.when(i == 0)
    def _():
        for head in heads:
            head(phase="first")

    @pl.when(i > 0)
    def _():
        for head in heads:
            head(phase="later")

    def live(blk):
        return (blk >= 0) & (jnp.max(carry_scr[...]) > EXP2_UNDERFLOW)

    def body(blk):
        for head in heads:
            head(phase="advance", blk=blk)
        return blk - 1

    lax.while_loop(live, body, 2 * i - 1)
    for head in heads:
        head(phase="finish")


def _sb_head(q_ref, k_ref, v_ref, sg_ref, upper_ref, o_ref, vt_scr, acc_scr, carry_scr,
             *, i, tq, tk, phase, blk=None):
    if phase == "finish":
        o_ref[...] = (acc_scr[...].T * sg_ref[...].astype(F32)).astype(BF16)
        return
    if phase == "first":
        for c in range(vt_scr.shape[0]):
            vt_scr[c] = v_ref[c * tk:(c + 1) * tk, :].T

    q_t = q_ref[...].T
    upper = upper_ref[...]

    def scores(blk, half):
        start = pl.multiple_of(blk * tk, tk)
        return _dot(k_ref[pl.ds(start, tk), :], q_t[:, half * tk:(half + 1) * tk])

    def stage(z, strict):
        soft = jnp.log2(1.0 + jnp.exp2(-jnp.abs(z)))
        log_beta = jnp.minimum(z, 0.0) - soft
        log_keep = log_beta - z
        if strict is not None:
            log_keep = jnp.where(strict, log_keep, 0.0)
        later = _dot(upper, log_keep.astype(BF16))
        return log_beta + later, jnp.sum(log_keep, axis=0, keepdims=True)

    def group(chains, carry, acc):
        raw = [scores(blk, half) for blk, _, half, _ in chains[:SCORES_AHEAD]]
        staged = []
        for n, (_, strict, _, _) in enumerate(chains):
            if n + SCORES_AHEAD < len(chains):
                blk, _, half, _ = chains[n + SCORES_AHEAD]
                raw.append(scores(blk, half))
            staged.append(stage(raw[n], strict))
        carry, acc = list(carry), list(acc)
        for (blk, strict, half, valid), (pre, total) in zip(chains, staged):
            w = jnp.exp2(pre + carry[half])
            if strict is not None:
                w = jnp.where(strict, w, 0.0)
            if valid is not None:
                w = jnp.where(valid, w, 0.0)
                total = jnp.where(valid, total, 0.0)
            acc[half] = acc[half] + _dot(vt_scr[blk], w.astype(BF16))
            carry[half] = carry[half] + total
        acc_scr[...] = jnp.concatenate(acc, axis=1)
        carry_scr[...] = jnp.concatenate(carry, axis=1)

    key = lax.broadcasted_iota(jnp.int32, (tk, tk), 0)
    qry = lax.broadcasted_iota(jnp.int32, (tk, tk), 1)
    strict = key < qry
    first = 2 * i
    zeros = ([jnp.zeros((1, tk), F32)] * 2, [jnp.zeros((HEAD_DIM, tk), F32)] * 2)

    if phase == "first":
        group([(1, strict, 1, None), (0, strict, 0, None), (0, None, 1, None)], *zeros)
    elif phase == "later":
        group([(first + 1, strict, 1, None), (first, strict, 0, None),
               (first, None, 1, None), (first - 1, None, 0, None)], *zeros)
    else:
        carry, acc = carry_scr[...], acc_scr[...]
        group([(blk, None, 1, None), (jnp.maximum(blk - 1, 0), None, 0, blk >= 1)],
              [carry[:, :tk], carry[:, tk:]], [acc[:, :tk], acc[:, tk:]])


def _sb_attention(qkvg, heads):
    b, s, _ = qkvg.shape
    tq = _tile(s, 512)
    tk = _tile(s, 256)
    assert tq == 2 * tk, "the kernel walks the query block as two halves of tk queries"
    groups, width = heads // HEADS_PER_STEP, HEADS_PER_STEP * HEAD_DIM
    upper = (lax.broadcasted_iota(jnp.int32, (tk, tk), 1)
             > lax.broadcasted_iota(jnp.int32, (tk, tk), 0)).astype(BF16)
    return pl.pallas_call(
        functools.partial(_sb_attn_kernel, tq=tq, tk=tk),
        out_shape=jax.ShapeDtypeStruct((b, s, heads * HEAD_DIM), BF16),
        grid=(b, groups, s // tq),
        in_specs=[
            pl.BlockSpec((None, tq, width), lambda bi, h, i: (bi, i, h)),
            pl.BlockSpec((None, s, width), lambda bi, h, i: (bi, 0, groups + h)),
            pl.BlockSpec((None, s, width), lambda bi, h, i: (bi, 0, 2 * groups + h)),
            pl.BlockSpec((None, tq, width), lambda bi, h, i: (bi, i, 3 * groups + h)),
            pl.BlockSpec((tk, tk), lambda bi, h, i: (0, 0)),
        ],
        out_specs=pl.BlockSpec((None, tq, width), lambda bi, h, i: (bi, i, h)),
        scratch_shapes=[pltpu.VMEM((HEADS_PER_STEP, s // tk, HEAD_DIM, tk), BF16),
                        pltpu.VMEM((HEADS_PER_STEP, HEAD_DIM, tq), F32),
                        pltpu.VMEM((HEADS_PER_STEP, 1, tq), F32)],
        compiler_params=_params("parallel", "parallel", "arbitrary"),
        name="sb_attn",
    )(qkvg, qkvg, qkvg, qkvg, upper)


def _out_kernel(og_ref, x_ref, p_ref, wout_ref, gpost_ref, wpp_ref, wpg_ref, gnext_ref,
                xo_ref, *next_refs):
    y = _dot(og_ref[...], wout_ref[...])
    x1 = x_ref[...] + y * _rms_scale(y) * gpost_ref[...]
    gate = _dot(x1.astype(BF16), wpg_ref[...])
    emb = _dot(p_ref[...].astype(BF16), wpp_ref[...])
    x2 = x1 + emb * _sigmoid(gate)
    xo_ref[...] = x2
    if next_refs:
        normed = x2 * _rms_scale(x2)
        for n, ref in enumerate(next_refs):
            ref[...] = (normed * gnext_ref[n:n + 1, :]).astype(BF16)


def _out_proj(og2d, x2d, p3d, layer, w_out, g_post, w_pp, w_pg, next_gains):
    t, d = x2d.shape
    width = og2d.shape[1]
    ple = p3d.shape[2]
    tm = _tile(t, 512)
    n_next = len(next_gains)
    gnext = jnp.stack(next_gains) if n_next else jnp.zeros((1, d), F32)
    row = lambda i: (i, 0)
    outs = pl.pallas_call(
        _out_kernel,
        out_shape=[jax.ShapeDtypeStruct((t, d), F32)]
        + [jax.ShapeDtypeStruct((t, d), BF16)] * n_next,
        grid=(t // tm,),
        in_specs=[
            pl.BlockSpec((tm, width), row),
            pl.BlockSpec((tm, d), row),
            pl.BlockSpec((None, tm, ple), lambda i: (layer, i, 0)),
            _resident((width, d)),
            _resident((1, d)),
            _resident((ple, d)),
            _resident((d, d)),
            _resident(gnext.shape),
        ],
        out_specs=[pl.BlockSpec((tm, d), row)] * (1 + n_next),
        compiler_params=_params("parallel"),
        name="out_proj",
    )(og2d, x2d, p3d, w_out, g_post.reshape(1, d), w_pp, w_pg, gnext)
    return outs[0], list(outs[1:])


def _kv_kernel(h_ref, wd_ref, gl_ref, wuk_ref, wuvt_ref, cos_ref, sin_ref, k_ref, vt_ref,
               *, rank, heads):
    c = _dot(h_ref[...], wd_ref[...])
    ckv = c[:, :rank]
    k_rope = c[:, rank:rank + ROPE_DIM]
    k_rot = c[:, rank + 128:rank + 128 + ROPE_DIM]
    latent = (ckv * _rms_scale(ckv) * gl_ref[...]).astype(BF16)
    k_nope = _dot(latent, wuk_ref[...]).astype(BF16)
    vt_ref[...] = _dot_nt(wuvt_ref[...], latent).astype(BF16)
    roped = (k_rope * cos_ref[...] + k_rot * sin_ref[...]).astype(BF16)
    for h in range(heads):
        k_ref[h, :, 0:HEAD_DIM] = k_nope[:, h * HEAD_DIM:(h + 1) * HEAD_DIM]
        k_ref[h, :, HEAD_DIM:QK_DIM] = roped


def _kv_side(hkv, wd, g_latent, w_uk, w_uv_t, cos, sin, heads):
    b, s, d = hkv.shape
    rank = w_uk.shape[0]
    tm = _tile(s, 512)
    row = lambda bi, i: (bi, i, 0)
    return pl.pallas_call(
        functools.partial(_kv_kernel, rank=rank, heads=heads),
        out_shape=[jax.ShapeDtypeStruct((b, heads, s, QK_DIM), BF16),
                   jax.ShapeDtypeStruct((b, heads * HEAD_DIM, s), BF16)],
        grid=(b, s // tm),
        in_specs=[
            pl.BlockSpec((None, tm, d), row),
            _resident(wd.shape),
            _resident((1, rank)),
            _resident(w_uk.shape),
            _resident(w_uv_t.shape),
            pl.BlockSpec((tm, ROPE_DIM), lambda bi, i: (i, 0)),
            pl.BlockSpec((tm, ROPE_DIM), lambda bi, i: (i, 0)),
        ],
        out_specs=[pl.BlockSpec((None, heads, tm, QK_DIM), lambda bi, i: (bi, 0, i, 0)),
                   pl.BlockSpec((None, heads * HEAD_DIM, tm), lambda bi, i: (bi, 0, i))],
        compiler_params=_params("parallel", "parallel"),
        name="kv_side",
    )(hkv, wd, g_latent.reshape(1, rank), w_uk, w_uv_t, cos, sin)


def _q_kernel(h_ref, win_ref, gq_ref, wnt_ref, wrt_ref, wrott_ref, cost_ref, sint_ref,
              qt_ref, sg_ref, *, rank, heads, scale):
    c = _dot(h_ref[...], win_ref[...])
    cq = c[:, :rank]
    gate = c[:, rank:]
    sg_ref[...] = (gate * _sigmoid(gate)).astype(BF16)
    latent = (cq * _rms_scale(cq) * gq_ref[...]).astype(BF16)
    nope_t = (_dot_nt(wnt_ref[...], latent) * scale).astype(BF16)
    cos_t = jnp.tile(cost_ref[...], (heads, 1))
    sin_t = jnp.tile(sint_ref[...], (heads, 1))
    roped_t = ((_dot_nt(wrt_ref[...], latent) * cos_t + _dot_nt(wrott_ref[...], latent) * sin_t)
               * scale).astype(BF16)
    for h in range(heads):
        qt_ref[h, 0:HEAD_DIM, :] = nope_t[h * HEAD_DIM:(h + 1) * HEAD_DIM, :]
        qt_ref[h, HEAD_DIM:QK_DIM, :] = roped_t[h * ROPE_DIM:(h + 1) * ROPE_DIM, :]


def _q_side(h, w_in, g_latent, w_n_t, w_r_t, w_rot_t, cos_t, sin_t, heads):
    b, s, d = h.shape
    rank = w_n_t.shape[1]
    width = w_in.shape[1] - rank
    tm = _tile(s, 512)
    row = lambda bi, i: (bi, i, 0)
    return pl.pallas_call(
        functools.partial(_q_kernel, rank=rank, heads=heads, scale=QK_DIM ** -0.5 * LOG2_E),
        out_shape=[jax.ShapeDtypeStruct((b, heads, QK_DIM, s), BF16),
                   jax.ShapeDtypeStruct((b, s, width), BF16)],
        grid=(b, s // tm),
        in_specs=[
            pl.BlockSpec((None, tm, d), row),
            _resident(w_in.shape),
            _resident((1, rank)),
            _resident(w_n_t.shape),
            _resident(w_r_t.shape),
            _resident(w_rot_t.shape),
            pl.BlockSpec((ROPE_DIM, tm), lambda bi, i: (0, i)),
            pl.BlockSpec((ROPE_DIM, tm), lambda bi, i: (0, i)),
        ],
        out_specs=[pl.BlockSpec((None, heads, QK_DIM, tm), lambda bi, i: (bi, 0, 0, i)),
                   pl.BlockSpec((None, tm, width), row)],
        compiler_params=_params("parallel", "parallel"),
        name="q_side",
    )(h, w_in, g_latent.reshape(1, rank), w_n_t, w_r_t, w_rot_t, cos_t, sin_t)


def _mla_attn_kernel(qt_ref, k_ref, vt_ref, sg_ref, o_ref, s_scr, p_scr, acc_scr, *, tq, tk):
    i = pl.program_id(2)
    heads = []
    for hh in range(HEADS_PER_STEP):
        cols = pl.ds(hh * HEAD_DIM, HEAD_DIM)
        heads.append(functools.partial(
            _mla_head, qt_ref.at[hh], k_ref.at[hh], vt_ref.at[cols, :], sg_ref.at[:, cols],
            o_ref.at[:, cols], s_scr.at[hh], p_scr.at[hh], acc_scr.at[hh], tq=tq, tk=tk))

    nblk = (i + 1) * (tq // tk)
    kchunk = lax.broadcasted_iota(jnp.int32, (tk, tq), 0) // CHUNK
    qchunk = lax.broadcasted_iota(jnp.int32, (tk, tq), 1) // CHUNK
    first_diag = kchunk <= qchunk
    second_diag = (kchunk + tk // CHUNK) <= qchunk
    row = jnp.zeros((1, tq), F32)
    state = ((row, row), (jnp.full((1, tq), NEG_INF, F32), row))
    everything = dict(merge=True, softmax=True, score=True)

    def run(first_step):
        fns = [head() for head in heads]

        def all_heads(n, parity, sts, **kw):
            return [step(n, parity, *st, **kw) for (step, _), st in zip(fns, sts)]

        sts = all_heads(0, 0, [state] * len(fns), merge=False, softmax=False, score=True)
        if first_step:
            sts = all_heads(1, 1, sts, merge=False, softmax=True, score=True, mask=first_diag)
            sts = all_heads(2, 0, sts, merge=True, softmax=True, score=False, mask=second_diag)
            sts = all_heads(3, 1, sts, merge=True, softmax=False, score=False)
        else:
            sts = all_heads(1, 1, sts, merge=False, softmax=True, score=True)
            sts = all_heads(2, 0, sts, **everything)

            def pair(t, sts):
                sts = all_heads(3 + 2 * t, 1, sts, steady=True, **everything)
                return all_heads(4 + 2 * t, 0, sts, steady=True, **everything)

            sts = lax.fori_loop(0, (nblk - 4) // 2, pair, sts)
            sts = all_heads(nblk - 1, 1, sts, mask=first_diag, **everything)
            sts = all_heads(nblk, 0, sts, merge=True, softmax=True, score=False,
                            mask=second_diag)
            sts = all_heads(nblk + 1, 1, sts, merge=True, softmax=False, score=False)
        for (_, finish), st in zip(fns, sts):
            finish(st[1])

    pl.when(i == 0)(functools.partial(run, True))
    pl.when(i > 0)(functools.partial(run, False))


def _mla_head(qt_ref, k_ref, vt_ref, sg_ref, o_ref, s_scr, p_scr, acc_scr, *, tq, tk):
    q_t = qt_ref[...]
    acc_scr[...] = jnp.zeros_like(acc_scr)

    def key_start(n):
        return pl.multiple_of(n * tk, tk)

    def step(n, parity, stats, run, *, merge, softmax, score, mask=None, steady=False):
        if merge and steady:
            out = _dot(vt_ref[:, pl.ds(key_start(n - 2), tk)], p_scr[parity])
        if score:
            s_scr[parity] = _dot(k_ref[pl.ds(key_start(n), tk), :], q_t)
        if merge and not steady:
            out = _dot(vt_ref[:, pl.ds(key_start(n - 2), tk)], p_scr[parity])
        m_c, l_c = stats
        if softmax:
            s = s_scr[1 - parity]
            if mask is not None:
                s = jnp.where(mask, s, NEG_INF)
            m_next = jnp.max(s, axis=0, keepdims=True)
            prob = jnp.exp2(s - m_next)
            p_scr[1 - parity] = prob.astype(BF16)
            stats = (m_next, jnp.sum(prob, axis=0, keepdims=True))
        if merge:
            m_run, l_run = run
            m_new = jnp.maximum(m_run, m_c)
            alpha = jnp.exp2(m_run - m_new)
            weight = jnp.exp2(m_c - m_new)
            acc_scr[...] = acc_scr[...] * alpha + out * weight
            run = (m_new, l_run * alpha + l_c * weight)
        return stats, run

    def finish(run):
        out = (acc_scr[...] / run[1]).T
        o_ref[...] = (out * sg_ref[...].astype(F32)).astype(BF16)

    return step, finish


def _mla_attention(q_t, k_cat, v_t, sg):
    b, heads, _, s = q_t.shape
    tq = _tile(s, 512)
    tk = _tile(s, 256)
    assert tq == 2 * tk, "the pipeline below peels exactly two diagonal key blocks"
    width = HEADS_PER_STEP * HEAD_DIM
    return pl.pallas_call(
        functools.partial(_mla_attn_kernel, tq=tq, tk=tk),
        out_shape=jax.ShapeDtypeStruct((b, s, heads * HEAD_DIM), BF16),
        grid=(b, heads // HEADS_PER_STEP, s // tq),
        in_specs=[
            pl.BlockSpec((None, HEADS_PER_STEP, QK_DIM, tq), lambda bi, h, i: (bi, h, 0, i)),
            pl.BlockSpec((None, HEADS_PER_STEP, s, QK_DIM), lambda bi, h, i: (bi, h, 0, 0)),
            pl.BlockSpec((None, width, s), lambda bi, h, i: (bi, h, 0)),
            pl.BlockSpec((None, tq, width), lambda bi, h, i: (bi, i, h)),
        ],
        out_specs=pl.BlockSpec((None, tq, width), lambda bi, h, i: (bi, i, h)),
        scratch_shapes=[pltpu.VMEM((HEADS_PER_STEP, 2, tk, tq), F32),
                        pltpu.VMEM((HEADS_PER_STEP, 2, tk, tq), BF16),
                        pltpu.VMEM((HEADS_PER_STEP, HEAD_DIM, tq), F32)],
        compiler_params=_params("parallel", "parallel", "arbitrary"),
        name="mla_attn",
    )(q_t, k_cat, v_t, sg)


def _rotate_half_columns(w):
    half = ROPE_DIM // 2
    return jnp.concatenate([-w[..., half:], w[..., :half]], axis=-1)


def _rope_tables(s):
    inv = 1.0 / (ROPE_THETA ** (jnp.arange(0, ROPE_DIM, 2, dtype=F32) / ROPE_DIM))
    ang = jnp.arange(s, dtype=F32)[:, None] * inv[None, :]
    cos, sin = jnp.cos(ang), jnp.sin(ang)
    return jnp.tile(cos, (1, 2)), jnp.tile(sin, (1, 2))


def kernel(x, p, norm_pre, norm_post, w_in_a, w_out_a, w_in_b, q_latent_norm, w_uq, w_out_b,
           kv_norm, w_dkv, kv_latent_norm, w_uk, w_uv, w_ple_proj, w_ple_gate):
    b, s, d = x.shape
    t = b * s
    n_a, n_b = w_in_a.shape[0], w_in_b.shape[0]
    heads = d // HEAD_DIM
    q_rank = w_uq.shape[1]
    kv_rank = w_uk.shape[0]
    cos, sin = _rope_tables(s)

    x2d = x.reshape(t, d)
    p3d = p.reshape(p.shape[0], t, p.shape[-1])

    def finish_layer(layer, og, x2d, w_out):
        if layer + 1 < n_a:
            gains = [norm_pre[layer + 1]]
        elif layer + 1 == n_a and n_b:
            gains = [norm_pre[layer + 1], kv_norm]
        elif layer + 1 < n_a + n_b:
            gains = [norm_pre[layer + 1]]
        else:
            gains = []
        return _out_proj(og.reshape(t, -1), x2d, p3d, layer, w_out.astype(BF16),
                         norm_post[layer], w_ple_proj[layer].astype(BF16),
                         w_ple_gate[layer].astype(BF16), gains)

    h_next = None
    for i in range(n_a):
        qkvg = _proj_a(x2d, norm_pre[i], w_in_a[i])
        og = _sb_attention(qkvg.reshape(b, s, -1), heads)
        x2d, h_next = finish_layer(i, og, x2d, w_out_a[i])

    if n_b == 0:
        return x2d.reshape(b, s, d)
    if n_a == 0:
        raise NotImplementedError("a trunk without mixer-A layers is not supported")

    h_b, h_kv = h_next
    wd = jnp.concatenate([
        w_dkv[:, :kv_rank],
        jnp.pad(w_dkv[:, kv_rank:], ((0, 0), (0, 128 - ROPE_DIM))),
        jnp.pad(_rotate_half_columns(w_dkv[:, kv_rank:]), ((0, 0), (0, 128 - ROPE_DIM))),
    ], axis=1).astype(BF16)
    k_cat, v_t = _kv_side(h_kv.reshape(b, s, d), wd, kv_latent_norm, w_uk.astype(BF16),
                          w_uv.T.astype(BF16), cos, sin, heads)

    for j in range(n_b):
        layer = n_a + j
        wq = w_uq[j].reshape(q_rank, heads, QK_DIM)
        w_n_t = wq[:, :, :HEAD_DIM].reshape(q_rank, heads * HEAD_DIM).T.astype(BF16)
        w_r_t = wq[:, :, HEAD_DIM:].reshape(q_rank, heads * ROPE_DIM).T.astype(BF16)
        w_rot_t = _rotate_half_columns(wq[:, :, HEAD_DIM:]).reshape(
            q_rank, heads * ROPE_DIM).T.astype(BF16)
        q_t, sg = _q_side(h_b.reshape(b, s, d), w_in_b[j].astype(BF16), q_latent_norm[j],
                          w_n_t, w_r_t, w_rot_t, cos.T, sin.T, heads)
        og = _mla_attention(q_t, k_cat, v_t, sg)
        x2d, h_next = finish_layer(layer, og, x2d, w_out_b[j])
        if h_next:
            h_b = h_next[0]
    return x2d.reshape(b, s, d)
```
